```python
import math
import jax, jax.numpy as jnp
from jax import lax
import numpy as np

D_MODEL = 1024
BATCH = 1
SEQ = 16384
DEPTH = 2

GRID_W = 64
CTX_LEN = 256
N_Q_HEADS = 8
N_KV_HEADS = 2
GQA_GROUP = N_Q_HEADS // N_KV_HEADS
HEAD_DIM = 128
ATTN_WIDTH = N_Q_HEADS * HEAD_DIM
KV_WIDTH = N_KV_HEADS * HEAD_DIM
ROPE_THETA = 10000.0
ROPE_FREQS = HEAD_DIM // 4
Q_BLOCK = 128
ATTN_SCALE = 1.0 / math.sqrt(HEAD_DIM)
GMLP_WIDTH = D_MODEL
GMLP_GROUPS = 4
GMLP_GROUP_DIM = GMLP_WIDTH // GMLP_GROUPS
CHUNK = 128
CONV_WIDTH = D_MODEL
CONV_KERNEL = 31
CONV_PAD = CONV_KERNEL // 2
N_BRANCHES = 3
D_FF = 4 * D_MODEL
EPS = 1e-6
Q_OFF = 0
K_OFF = Q_OFF + ATTN_WIDTH
V_OFF = K_OFF + KV_WIDTH
GU_OFF = V_OFF + KV_WIDTH
GV_OFF = GU_OFF + GMLP_WIDTH
CG_OFF = GV_OFF + GMLP_WIDTH
GATE_OFF = CG_OFF + 2 * CONV_WIDTH
IN_COLS = GATE_OFF + N_BRANCHES * D_MODEL

kernel_name = 'hybrid_gqa_gmlp_conformer_dit_block'


def rms_norm(x, g):
    xf = x.astype(jnp.float32)
    y = xf * lax.rsqrt(jnp.mean(xf * xf, axis=-1, keepdims=True) + EPS)
    return (y * g.astype(jnp.float32)).astype(x.dtype)


def layer_norm(x, g, b):
    xf = x.astype(jnp.float32)
    mu = jnp.mean(xf, axis=-1, keepdims=True)
    var = jnp.mean(jnp.square(xf - mu), axis=-1, keepdims=True)
    y = (xf - mu) * lax.rsqrt(var + EPS)
    return (y * g.astype(jnp.float32) + b.astype(jnp.float32)).astype(x.dtype)


def modulate(h, shift, scale):
    return h * (1.0 + scale) + shift


def axial_rope_tables(n_rows):
    row = jnp.repeat(jnp.arange(n_rows), GRID_W).astype(jnp.float32)
    col = jnp.tile(jnp.arange(GRID_W), n_rows).astype(jnp.float32)
    inv = ROPE_THETA ** (-jnp.arange(ROPE_FREQS, dtype=jnp.float32) / ROPE_FREQS)
    ang = jnp.concatenate([row[:, None] * inv, col[:, None] * inv], axis=-1)
    return jnp.cos(ang), jnp.sin(ang)


def apply_rope(x, cos, sin):
    half = HEAD_DIM // 2
    x1, x2 = x[..., :half], x[..., half:]
    c = cos[None, :, None, :].astype(x.dtype)
    s = sin[None, :, None, :].astype(x.dtype)
    return jnp.concatenate([x1 * c - x2 * s, x2 * c + x1 * s], axis=-1)


def q_heads(p, q_norm_g):
    B, L, _ = p.shape
    return rms_norm(p[..., Q_OFF:K_OFF].reshape(B, L, N_Q_HEADS, HEAD_DIM), q_norm_g)


def kv_heads(pkv, k_norm_g):
    B, L, _ = pkv.shape
    k = rms_norm(pkv[..., :KV_WIDTH].reshape(B, L, N_KV_HEADS, HEAD_DIM), k_norm_g)
    v = pkv[..., KV_WIDTH:].reshape(B, L, N_KV_HEADS, HEAD_DIM)
    return k, v


def gqa_attend(q, k, v):
    s = jnp.einsum('bqhgd,bkhd->bhgqk', q, k).astype(jnp.float32) * ATTN_SCALE
    p = jax.nn.softmax(s, axis=-1).astype(v.dtype)
    return jnp.einsum('bhgqk,bkhd->bqhgd', p, v)


def latent_attention(q, k_all, v_all):
    B, S = q.shape[0], q.shape[1]
    nb = S // Q_BLOCK
    qb = q.reshape(B, nb, Q_BLOCK, N_KV_HEADS, GQA_GROUP, HEAD_DIM).transpose(1, 0, 2, 3, 4, 5)
    ob = lax.map(lambda qi: gqa_attend(qi, k_all, v_all), qb)
    return ob.transpose(1, 0, 2, 3, 4, 5).reshape(B, S, ATTN_WIDTH)


def chunk_spatial_gate(u, v, ws, bs):
    B, L, _ = v.shape
    nc = L // CHUNK
    vg = v.reshape(B, nc, CHUNK, GMLP_GROUPS, GMLP_GROUP_DIM)
    sv = jnp.einsum('gpq,bnqgc->bnpgc', ws, vg) + bs.T[None, None, :, :, None]
    return u * sv.reshape(B, L, GMLP_WIDTH)


def conformer_conv(a, conv_w, conv_b, norm_g, norm_b, w_o):
    gl = a[..., :CONV_WIDTH] * jax.nn.sigmoid(a[..., CONV_WIDTH:])
    y = lax.conv_general_dilated(gl, conv_w[:, None, :], window_strides=(1,),
                                 padding=[(CONV_PAD, CONV_PAD)],
                                 dimension_numbers=('NWC', 'WIO', 'NWC'),
                                 feature_group_count=CONV_WIDTH) + conv_b
    y = jax.nn.silu(layer_norm(y, norm_g, norm_b))
    return y @ w_o


def mixer_merge(p, attn, w_attn_o, gmlp_norm_g, gmlp_ws, gmlp_bs, w_gmlp_o,
                conv_w, conv_b, conv_norm_g, conv_norm_b, w_conv_o, w_out):
    a_out = attn @ w_attn_o
    gu = jax.nn.gelu(p[..., GU_OFF:GV_OFF])
    gv = rms_norm(jax.nn.gelu(p[..., GV_OFF:CG_OFF]), gmlp_norm_g)
    g_out = chunk_spatial_gate(gu, gv, gmlp_ws, gmlp_bs) @ w_gmlp_o
    c_out = conformer_conv(p[..., CG_OFF:GATE_OFF], conv_w, conv_b, conv_norm_g, conv_norm_b, w_conv_o)
    gates = jax.nn.sigmoid(p[..., GATE_OFF:])
    ga = gates[..., :D_MODEL]
    gg = gates[..., D_MODEL:2 * D_MODEL]
    gc = gates[..., 2 * D_MODEL:]
    return (ga * a_out + gg * g_out + gc * c_out) @ w_out


def sq_relu_mlp(h, w1, w2):
    return jnp.square(jax.nn.relu(h @ w1)) @ w2


def setup_inputs(seed: int = 0) -> dict:
    key = jax.random.key(seed)
    ks = jax.random.split(key, 32)
    f32 = jnp.float32

    def nrm(k, shape, scale):
        return jax.random.normal(k, shape, f32) * scale

    def gain(k, shape):
        return 1.0 + 0.1 * jax.random.normal(k, shape, f32)

    L = DEPTH
    return {
        'x': nrm(ks[0], (BATCH, SEQ, D_MODEL), 1.0),
        'c': nrm(ks[1], (BATCH, D_MODEL), 1.0),
        'ctx': nrm(ks[2], (BATCH, CTX_LEN, D_MODEL), 1.0),
        'c_ctx': nrm(ks[3], (D_MODEL,), 1.0),
        'ada_w': nrm(ks[4], (L, D_MODEL, 6 * D_MODEL), 0.5 * D_MODEL ** -0.5),
        'ada_b': nrm(ks[5], (L, 6 * D_MODEL), 0.02),
        'mix_pre_g': gain(ks[6], (L, D_MODEL)),
        'mix_post_g': gain(ks[7], (L, D_MODEL)),
        'w_in': nrm(ks[8], (L, D_MODEL, IN_COLS), D_MODEL ** -0.5),
        'q_norm_g': gain(ks[9], (L, HEAD_DIM)),
        'k_norm_g': gain(ks[10], (L, HEAD_DIM)),
        'w_attn_o': nrm(ks[11], (L, ATTN_WIDTH, D_MODEL), ATTN_WIDTH ** -0.5),
        'gmlp_norm_g': gain(ks[12], (L, GMLP_WIDTH)),
        'gmlp_ws': nrm(ks[13], (L, GMLP_GROUPS, CHUNK, CHUNK), CHUNK ** -0.5),
        'gmlp_bs': gain(ks[14], (L, GMLP_GROUPS, CHUNK)),
        'w_gmlp_o': nrm(ks[15], (L, GMLP_WIDTH, D_MODEL), GMLP_WIDTH ** -0.5),
        'conv_w': nrm(ks[16], (L, CONV_KERNEL, CONV_WIDTH), CONV_KERNEL ** -0.5),
        'conv_b': nrm(ks[17], (L, CONV_WIDTH), 0.02),
        'conv_norm_g': gain(ks[18], (L, CONV_WIDTH)),
        'conv_norm_b': nrm(ks[19], (L, CONV_WIDTH), 0.02),
        'w_conv_o': nrm(ks[20], (L, CONV_WIDTH, D_MODEL), CONV_WIDTH ** -0.5),
        'w_out': nrm(ks[21], (L, D_MODEL, D_MODEL), D_MODEL ** -0.5),
        'ffn_pre_g': gain(ks[22], (L, D_MODEL)),
        'ffn_post_g': gain(ks[23], (L, D_MODEL)),
        'w_ff1': nrm(ks[24], (L, D_MODEL, D_FF), D_MODEL ** -0.5),
        'w_ff2': nrm(ks[25], (L, D_FF, D_MODEL), D_FF ** -0.5),
    }


def reference(x, c, ctx, c_ctx, ada_w, ada_b, mix_pre_g, mix_post_g, w_in, q_norm_g, k_norm_g,
              w_attn_o, gmlp_norm_g, gmlp_ws, gmlp_bs, w_gmlp_o, conv_w, conv_b, conv_norm_g,
              conv_norm_b, w_conv_o, w_out, ffn_pre_g, ffn_post_g, w_ff1, w_ff2):
    B, S, _ = x.shape
    n_ctx = ctx.shape[1]
    ROWS = S // GRID_W
    cos, sin = axial_rope_tables(ROWS)
    xc = ctx
    for l in range(DEPTH):
        last = l == DEPTH - 1
        mod = (jax.nn.silu(c) @ ada_w[l] + ada_b[l])[:, None, :]
        sh1, sc1, g1, sh2, sc2, g2 = jnp.split(mod, 6, axis=-1)
        modc = jax.nn.silu(c_ctx) @ ada_w[l] + ada_b[l]
        shc1, scc1, gc1, shc2, scc2, gc2 = jnp.split(modc, 6, axis=-1)

        h = modulate(rms_norm(x, mix_pre_g[l]), sh1, sc1)
        hc = modulate(rms_norm(xc, mix_pre_g[l]), shc1, scc1)
        p = h @ w_in[l]
        q = apply_rope(q_heads(p, q_norm_g[l]), cos, sin)
        k, v = kv_heads(p[..., K_OFF:GU_OFF], k_norm_g[l])
        k = apply_rope(k, cos, sin)
        if last:
            kc, vc = kv_heads(hc @ w_in[l][:, K_OFF:GU_OFF], k_norm_g[l])
        else:
            pc = hc @ w_in[l]
            qc = q_heads(pc, q_norm_g[l])
            kc, vc = kv_heads(pc[..., K_OFF:GU_OFF], k_norm_g[l])
        k_all = jnp.concatenate([kc, k], axis=1)
        v_all = jnp.concatenate([vc, v], axis=1)
        attn = latent_attention(q.reshape(B, S, N_KV_HEADS, GQA_GROUP, HEAD_DIM), k_all, v_all)
        y = mixer_merge(p, attn, w_attn_o[l], gmlp_norm_g[l], gmlp_ws[l], gmlp_bs[l], w_gmlp_o[l],
                        conv_w[l], conv_b[l], conv_norm_g[l], conv_norm_b[l], w_conv_o[l], w_out[l])
        x = x + g1 * rms_norm(y, mix_post_g[l])

        h2 = modulate(rms_norm(x, ffn_pre_g[l]), sh2, sc2)
        x = x + g2 * rms_norm(sq_relu_mlp(h2, w_ff1[l], w_ff2[l]), ffn_post_g[l])

        if not last:
            attn_c = gqa_attend(qc.reshape(B, n_ctx, N_KV_HEADS, GQA_GROUP, HEAD_DIM), kc, vc)
            yc = mixer_merge(pc, attn_c.reshape(B, n_ctx, ATTN_WIDTH), w_attn_o[l], gmlp_norm_g[l],
                             gmlp_ws[l], gmlp_bs[l], w_gmlp_o[l], conv_w[l], conv_b[l],
                             conv_norm_g[l], conv_norm_b[l], w_conv_o[l], w_out[l])
            xc = xc + gc1 * rms_norm(yc, mix_post_g[l])
            hc2 = modulate(rms_norm(xc, ffn_pre_g[l]), shc2, scc2)
            xc = xc + gc2 * rms_norm(sq_relu_mlp(hc2, w_ff1[l], w_ff2[l]), ffn_post_g[l])
    return x
```

```python
import functools
import math

import jax
import jax.numpy as jnp
from jax import lax
from jax.experimental import pallas as pl
from jax.experimental.pallas import tpu as pltpu

D_MODEL = 1024
DEPTH = 2
GRID_W = 64
N_Q_HEADS = 8
N_KV_HEADS = 2
GQA_GROUP = N_Q_HEADS // N_KV_HEADS
HEAD_DIM = 128
ATTN_WIDTH = N_Q_HEADS * HEAD_DIM
KV_WIDTH = N_KV_HEADS * HEAD_DIM
ROPE_THETA = 10000.0
ROPE_FREQS = HEAD_DIM // 4
ATTN_SCALE = 1.0 / math.sqrt(HEAD_DIM)
GMLP_WIDTH = D_MODEL
GMLP_GROUPS = 4
GMLP_GROUP_DIM = GMLP_WIDTH // GMLP_GROUPS
CHUNK = 128
CONV_WIDTH = D_MODEL
CONV_KERNEL = 31
CONV_PAD = CONV_KERNEL // 2
D_FF = 4 * D_MODEL
EPS = 1e-6
Q_OFF = 0
K_OFF = Q_OFF + ATTN_WIDTH
V_OFF = K_OFF + KV_WIDTH
GU_OFF = V_OFF + KV_WIDTH
GV_OFF = GU_OFF + GMLP_WIDTH
CG_OFF = GV_OFF + GMLP_WIDTH
GATE_OFF = CG_OFF + 2 * CONV_WIDTH
IN_COLS = GATE_OFF + 3 * D_MODEL

LOG2E = 1.4426950408889634
GELU_C = math.sqrt(2.0 / math.pi)
HALO = 16
MOD_ROWS = 8
VMEM_LIMIT = 56 * 1024 * 1024

F32 = jnp.float32
BF16 = jnp.bfloat16


def _sigmoid(x):
    return 1.0 / (1.0 + jnp.exp(-x))


def _gelu_tanh(x):
    return x * (0.5 * (1.0 + jnp.tanh(GELU_C * (x + 0.044715 * (x * x * x)))))


def _rms(x, g):
    return x * lax.rsqrt(jnp.mean(x * x, axis=-1, keepdims=True) + EPS) * g


def _const_spec(shape):
    n = len(shape)
    return pl.BlockSpec(shape, lambda *_: (0,) * n, pipeline_mode=pl.Buffered(1))


def _mod_kernel(cc_ref, w_ref, b_ref, o_ref):
    cc = cc_ref[...]
    a = cc * _sigmoid(cc)
    o_ref[...] = jnp.dot(a, w_ref[...], preferred_element_type=F32,
                         precision=lax.Precision.HIGHEST) + b_ref[...]


def _mod_call(cc, ada_w, ada_b):
    nblk = 4
    bw = 6 * D_MODEL // nblk
    return pl.pallas_call(
        _mod_kernel,
        out_shape=jax.ShapeDtypeStruct((DEPTH, MOD_ROWS, 6 * D_MODEL), F32),
        grid=(DEPTH, nblk),
        in_specs=[
            pl.BlockSpec((MOD_ROWS, D_MODEL), lambda l, j: (0, 0)),
            pl.BlockSpec((None, D_MODEL, bw), lambda l, j: (l, 0, j)),
            pl.BlockSpec((None, 1, bw), lambda l, j: (l, 0, j)),
        ],
        out_specs=pl.BlockSpec((None, MOD_ROWS, bw), lambda l, j: (l, 0, j)),
        compiler_params=pltpu.CompilerParams(
            dimension_semantics=("parallel", "parallel"), vmem_limit_bytes=VMEM_LIMIT),
        name="adaln_table",
    )(cc, ada_w, ada_b.reshape(DEPTH, 1, 6 * D_MODEL))


def _inproj_kernel(*refs, tm, row, rope):
    if rope:
        (x_ref, mod_ref, pre_g_ref, w_ref, qg_ref, kg_ref, cos_ref, sin_ref, gng_ref, ws_ref, bst_ref,
         wgo_ref, q_ref, k_ref, v_ref, gl_ref, u_ref, ga_ref, gc_ref, h_sc, gp_sc) = refs
    else:
        (x_ref, mod_ref, pre_g_ref, w_ref, qg_ref, kg_ref, gng_ref, ws_ref, bst_ref,
         wgo_ref, q_ref, k_ref, v_ref, gl_ref, u_ref, ga_ref, gc_ref, h_sc, gp_sc) = refs
        cos_ref = sin_ref = None

    sh = mod_ref[row:row + 1, 0:D_MODEL]
    sc = mod_ref[row:row + 1, D_MODEL:2 * D_MODEL]
    h = _rms(x_ref[...], pre_g_ref[...]) * (1.0 + sc) + sh
    h_sc[...] = h.astype(BF16)

    def proj(lo, hi):
        return jnp.dot(h_sc[...], w_ref[:, lo:hi], preferred_element_type=F32)

    def head_norm_rope(t, g):
        t = _rms(t, g)
        if rope:
            t = t * cos_ref[...] + pltpu.roll(t, HEAD_DIM // 2, 1) * sin_ref[...]
        return t

    pq = proj(Q_OFF, K_OFF)
    for hh in range(N_Q_HEADS):
        t = head_norm_rope(pq[:, hh * HEAD_DIM:(hh + 1) * HEAD_DIM], qg_ref[...])
        q_ref[hh] = (t * (ATTN_SCALE * LOG2E)).astype(BF16)

    pk = proj(K_OFF, V_OFF)
    for hh in range(N_KV_HEADS):
        k_ref[hh] = head_norm_rope(pk[:, hh * HEAD_DIM:(hh + 1) * HEAD_DIM], kg_ref[...]).astype(BF16)
    pv = proj(V_OFF, GU_OFF)
    for hh in range(N_KV_HEADS):
        v_ref[hh] = pv[:, hh * HEAD_DIM:(hh + 1) * HEAD_DIM].astype(BF16)

    gu = _gelu_tanh(proj(GU_OFF, GV_OFF))
    gv = _rms(_gelu_tanh(proj(GV_OFF, CG_OFF)), gng_ref[...]).astype(BF16)
    for c in range(tm // CHUNK):
        rs = slice(c * CHUNK, (c + 1) * CHUNK)
        for g in range(GMLP_GROUPS):
            cs = slice(g * GMLP_GROUP_DIM, (g + 1) * GMLP_GROUP_DIM)
            sv = jnp.dot(ws_ref[g], gv[rs, cs], preferred_element_type=F32) + bst_ref[:, g:g + 1]
            gp_sc[rs, cs] = (gu[rs, cs] * sv).astype(BF16)
    g_out = jnp.dot(gp_sc[...], wgo_ref[...], preferred_element_type=F32)

    pg = proj(GATE_OFF, IN_COLS)
    ga_ref[...] = _sigmoid(pg[:, 0:D_MODEL]).astype(BF16)
    u_ref[...] = (_sigmoid(pg[:, D_MODEL:2 * D_MODEL]) * g_out).astype(BF16)
    gc_ref[...] = _sigmoid(pg[:, 2 * D_MODEL:3 * D_MODEL]).astype(BF16)

    pa = proj(CG_OFF, CG_OFF + CONV_WIDTH)
    pb = proj(CG_OFF + CONV_WIDTH, GATE_OFF)
    gl_ref[...] = (pa * _sigmoid(pb)).astype(BF16)


def _inproj_call(x, mod, pre_g, w_in, qg, kg, cos2, sin2, gng, ws, bst, wgo, *, tm, row):
    n_tok = x.shape[0]
    rope = cos2 is not None
    row_spec = pl.BlockSpec((tm, D_MODEL), lambda i: (i, 0))
    head_tab = pl.BlockSpec((tm, HEAD_DIM), lambda i: (i, 0))
    in_specs = [row_spec, _const_spec((MOD_ROWS, 6 * D_MODEL)), _const_spec((1, D_MODEL)),
                _const_spec((D_MODEL, IN_COLS)), _const_spec((1, HEAD_DIM)), _const_spec((1, HEAD_DIM))]
    args = [x, mod, pre_g, w_in, qg, kg]
    if rope:
        in_specs += [head_tab, head_tab]
        args += [cos2, sin2]
    in_specs += [_const_spec((1, D_MODEL)), _const_spec((GMLP_GROUPS, CHUNK, CHUNK)),
                 _const_spec((CHUNK, GMLP_GROUPS)), _const_spec((D_MODEL, D_MODEL))]
    args += [gng, ws, bst, wgo]

    def heads(n):
        return (jax.ShapeDtypeStruct((n, n_tok, HEAD_DIM), BF16),
                pl.BlockSpec((n, tm, HEAD_DIM), lambda i: (0, i, 0)))

    wide = (jax.ShapeDtypeStruct((n_tok, D_MODEL), BF16), row_spec)
    outs = [heads(N_Q_HEADS), heads(N_KV_HEADS), heads(N_KV_HEADS), wide, wide, wide, wide]
    return pl.pallas_call(
        functools.partial(_inproj_kernel, tm=tm, row=row, rope=rope),
        out_shape=[o[0] for o in outs],
        grid=(n_tok // tm,),
        in_specs=in_specs,
        out_specs=[o[1] for o in outs],
        scratch_shapes=[pltpu.VMEM((tm, D_MODEL), BF16), pltpu.VMEM((tm, D_MODEL), BF16)],
        compiler_params=pltpu.CompilerParams(
            dimension_semantics=("parallel",), vmem_limit_bytes=VMEM_LIMIT),
        name="inproj_rope" if rope else "inproj_ctx",
    )(*args)


def _flash_kernel(*refs, tq, tk, n_chunks):
    if n_chunks:
        q_ref, kc_ref, vc_ref, k_ref, v_ref, o_ref, m_sc, l_sc, acc_sc = refs
    else:
        q_ref, kc_ref, vc_ref, o_ref, m_sc, l_sc, acc_sc = refs
    rows = GQA_GROUP * tq
    q = q_ref[...].reshape(rows, HEAD_DIM)

    def scores(k):
        return lax.dot_general(q, k, (((1,), (1,)), ((), ())), preferred_element_type=F32)

    s = scores(kc_ref[...])
    m = jnp.max(s, axis=1, keepdims=True)
    p = jnp.exp2(s - m)
    m_sc[...] = jnp.broadcast_to(m, (rows, HEAD_DIM))
    l_sc[...] = jnp.broadcast_to(jnp.sum(p, axis=1, keepdims=True), (rows, HEAD_DIM))
    acc_sc[...] = jnp.dot(p.astype(BF16), vc_ref[...], preferred_element_type=F32)

    if n_chunks:
        def body(c, carry):
            off = pl.multiple_of(c * tk, tk)
            s = scores(k_ref[pl.ds(off, tk), :])
            m_prev = m_sc[...]
            m_new = jnp.maximum(m_prev, jnp.max(s, axis=1, keepdims=True))
            alpha = jnp.exp2(m_prev - m_new)
            p = jnp.exp2(s - m_new[:, 0:1])
            l_sc[...] = alpha * l_sc[...] + jnp.sum(p, axis=1, keepdims=True)
            acc_sc[...] = alpha * acc_sc[...] + jnp.dot(
                p.astype(BF16), v_ref[pl.ds(off, tk), :], preferred_element_type=F32)
            m_sc[...] = m_new
            return carry

        lax.fori_loop(0, n_chunks, body, 0)

    o = acc_sc[...] / l_sc[...]
    for g in range(GQA_GROUP):
        o_ref[:, g * HEAD_DIM:(g + 1) * HEAD_DIM] = o[g * tq:(g + 1) * tq].astype(BF16)


def _flash_call(q, kc, vc, k, v, *, tq, tk):
    lq = q.shape[1]
    lc = kc.shape[1]
    n_chunks = 0 if k is None else k.shape[1] // tk
    rows = GQA_GROUP * tq
    in_specs = [pl.BlockSpec((GQA_GROUP, tq, HEAD_DIM), lambda h, i: (h, i, 0)),
                pl.BlockSpec((None, lc, HEAD_DIM), lambda h, i: (h, 0, 0)),
                pl.BlockSpec((None, lc, HEAD_DIM), lambda h, i: (h, 0, 0))]
    args = [q, kc, vc]
    if n_chunks:
        s_len = k.shape[1]
        in_specs += [pl.BlockSpec((None, s_len, HEAD_DIM), lambda h, i: (h, 0, 0)),
                     pl.BlockSpec((None, s_len, HEAD_DIM), lambda h, i: (h, 0, 0))]
        args += [k, v]
    return pl.pallas_call(
        functools.partial(_flash_kernel, tq=tq, tk=tk, n_chunks=n_chunks),
        out_shape=jax.ShapeDtypeStruct((lq, ATTN_WIDTH), BF16),
        grid=(N_KV_HEADS, lq // tq),
        in_specs=in_specs,
        out_specs=pl.BlockSpec((tq, GQA_GROUP * HEAD_DIM), lambda h, i: (i, h)),
        scratch_shapes=[pltpu.VMEM((rows, HEAD_DIM), F32), pltpu.VMEM((rows, HEAD_DIM), F32),
                        pltpu.VMEM((rows, HEAD_DIM), F32)],
        compiler_params=pltpu.CompilerParams(
            dimension_semantics=("parallel", "parallel"), vmem_limit_bytes=VMEM_LIMIT),
        name="flash_latent" if n_chunks else "flash_ctx",
    )(*args)


def _post_kernel(x_ref, attn_ref, glp_ref, gl_ref, gln_ref, u_ref, ga_ref, gc_ref, mod_ref,
                 post_g_ref, fpre_g_ref, fpost_g_ref, cw_ref, cb_ref, lng_ref, lnb_ref,
                 wao_ref, wco_ref, wout_ref, w1_ref, w2_ref, xo_ref, win_sc, conv_sc, *, tm, row):
    i = pl.program_id(0)
    n = pl.num_programs(0)

    keep_prev = jnp.where(i > 0, 1.0, 0.0).astype(F32)
    keep_next = jnp.where(i < n - 1, 1.0, 0.0).astype(F32)
    win_sc[0:HALO, :] = glp_ref[...].astype(F32) * keep_prev
    win_sc[HALO:HALO + tm, :] = gl_ref[...].astype(F32)
    win_sc[HALO + tm:HALO + tm + HALO, :] = gln_ref[...].astype(F32) * keep_next
    rb = 32
    for r in range(tm // rb):
        for cblk in range(D_MODEL // 128):
            cs = slice(cblk * 128, (cblk + 1) * 128)
            acc = jnp.broadcast_to(cb_ref[:, cs], (rb, 128))
            for kk in range(CONV_KERNEL):
                start = r * rb + kk + HALO - CONV_PAD
                acc = acc + cw_ref[kk:kk + 1, cs] * win_sc[start:start + rb, cs]
            conv_sc[r * rb:(r + 1) * rb, cs] = acc
    y = conv_sc[...]
    mu = jnp.mean(y, axis=-1, keepdims=True)
    yc = y - mu
    var = jnp.mean(yc * yc, axis=-1, keepdims=True)
    y = yc * lax.rsqrt(var + EPS) * lng_ref[...] + lnb_ref[...]
    y = y * _sigmoid(y)
    c_out = jnp.dot(y.astype(BF16), wco_ref[...], preferred_element_type=F32)

    a_out = jnp.dot(attn_ref[...], wao_ref[...], preferred_element_type=F32)
    merged = (ga_ref[...].astype(F32) * a_out + u_ref[...].astype(F32)
              + gc_ref[...].astype(F32) * c_out)
    ymix = jnp.dot(merged.astype(BF16), wout_ref[...], preferred_element_type=F32)

    def mod(j):
        return mod_ref[row:row + 1, j * D_MODEL:(j + 1) * D_MODEL]

    x1 = x_ref[...] + mod(2) * _rms(ymix, post_g_ref[...])
    h2 = (_rms(x1, fpre_g_ref[...]) * (1.0 + mod(4)) + mod(3)).astype(BF16)
    f = jnp.dot(h2, w1_ref[...], preferred_element_type=F32)
    f = jnp.maximum(f, 0.0)
    f = (f * f).astype(BF16)
    f2 = jnp.dot(f, w2_ref[...], preferred_element_type=F32)
    xo_ref[...] = x1 + mod(5) * _rms(f2, fpost_g_ref[...])


def _post_call(x, attn, gl, u, ga, gc, mod, post_g, fpre_g, fpost_g, cw, cb, lng, lnb,
               wao, wco, wout, w1, w2, *, tm, row):
    n_tok = x.shape[0]
    hb = tm // HALO
    n_hb = n_tok // HALO
    row_spec = pl.BlockSpec((tm, D_MODEL), lambda i: (i, 0))
    prev_spec = pl.BlockSpec((HALO, D_MODEL), lambda i: (jnp.maximum(i * hb - 1, 0), 0))
    next_spec = pl.BlockSpec((HALO, D_MODEL), lambda i: (jnp.minimum((i + 1) * hb, n_hb - 1), 0))
    vec = _const_spec((1, D_MODEL))
    sq = _const_spec((D_MODEL, D_MODEL))
    in_specs = [row_spec, row_spec, prev_spec, row_spec, next_spec, row_spec, row_spec, row_spec,
                _const_spec((MOD_ROWS, 6 * D_MODEL)), vec, vec, vec,
                _const_spec((CONV_KERNEL, D_MODEL)), vec, vec, vec,
                sq, sq, sq, _const_spec((D_MODEL, D_FF)), _const_spec((D_FF, D_MODEL))]
    return pl.pallas_call(
        functools.partial(_post_kernel, tm=tm, row=row),
        out_shape=jax.ShapeDtypeStruct((n_tok, D_MODEL), F32),
        grid=(n_tok // tm,),
        in_specs=in_specs,
        out_specs=row_spec,
        scratch_shapes=[pltpu.VMEM((tm + 2 * HALO, D_MODEL), F32), pltpu.VMEM((tm, D_MODEL), F32)],
        compiler_params=pltpu.CompilerParams(
            dimension_semantics=("parallel",), vmem_limit_bytes=VMEM_LIMIT),
        name="merge_mlp",
    )(x, attn, gl, gl, gl, u, ga, gc, mod, post_g, fpre_g, fpost_g, cw, cb, lng, lnb,
      wao, wco, wout, w1, w2)


def _rope_tables(n_rows):
    row = jnp.repeat(jnp.arange(n_rows), GRID_W).astype(F32)
    col = jnp.tile(jnp.arange(GRID_W), n_rows).astype(F32)
    inv = ROPE_THETA ** (-jnp.arange(ROPE_FREQS, dtype=F32) / ROPE_FREQS)
    ang = jnp.concatenate([row[:, None] * inv, col[:, None] * inv], axis=-1)
    cos, sin = jnp.cos(ang), jnp.sin(ang)
    return jnp.concatenate([cos, cos], axis=-1), jnp.concatenate([-sin, sin], axis=-1)


def kernel(x, c, ctx, c_ctx, ada_w, ada_b, mix_pre_g, mix_post_g, w_in, q_norm_g, k_norm_g, w_attn_o,
           gmlp_norm_g, gmlp_ws, gmlp_bs, w_gmlp_o, conv_w, conv_b, conv_norm_g, conv_norm_b, w_conv_o,
           w_out, ffn_pre_g, ffn_post_g, w_ff1, w_ff2):
    batch, seq, _ = x.shape
    assert batch == 1 and c.shape[0] == 1 and ctx.shape[0] == 1
    n_ctx = ctx.shape[1]
    tm_lat, tm_ctx = 256, n_ctx
    tq, tk = 256, 512
    assert seq % GRID_W == 0 and seq % tm_lat == 0 and seq % tk == 0 and n_ctx % CHUNK == 0

    xl, xc = x[0], ctx[0]
    cc = jnp.zeros((MOD_ROWS, D_MODEL), F32).at[0].set(c[0]).at[1].set(c_ctx)
    mods = _mod_call(cc, ada_w, ada_b)
    cos2, sin2 = _rope_tables(seq // GRID_W)

    def vec(a):
        return a.reshape(1, -1)

    for l in range(DEPTH):
        last = l == DEPTH - 1
        mod = mods[l]
        inproj_w = (vec(mix_pre_g[l]), w_in[l].astype(BF16), vec(q_norm_g[l]), vec(k_norm_g[l]))
        gmlp_w = (vec(gmlp_norm_g[l]), gmlp_ws[l].astype(BF16), gmlp_bs[l].T, w_gmlp_o[l].astype(BF16))
        post_w = (vec(mix_post_g[l]), vec(ffn_pre_g[l]), vec(ffn_post_g[l]), conv_w[l], vec(conv_b[l]),
                  vec(conv_norm_g[l]), vec(conv_norm_b[l]), w_attn_o[l].astype(BF16),
                  w_conv_o[l].astype(BF16), w_out[l].astype(BF16), w_ff1[l].astype(BF16),
                  w_ff2[l].astype(BF16))

        q, k, v, gl, u, ga, gc = _inproj_call(xl, mod, *inproj_w, cos2, sin2, *gmlp_w, tm=tm_lat, row=0)
        qc, kc, vc, glc, uc, gac, gcc = _inproj_call(xc, mod, *inproj_w, None, None, *gmlp_w,
                                                     tm=tm_ctx, row=1)
        attn = _flash_call(q, kc, vc, k, v, tq=tq, tk=tk)
        if not last:
            attn_c = _flash_call(qc, kc, vc, None, None, tq=n_ctx, tk=tk)
            xc = _post_call(xc, attn_c, glc, uc, gac, gcc, mod, *post_w, tm=tm_ctx, row=1)
        xl = _post_call(xl, attn, gl, u, ga, gc, mod, *post_w, tm=tm_lat, row=0)
    return xl[None]
```

```python
import functools
import math

import jax
import jax.numpy as jnp
from jax import lax
from jax.experimental import pallas as pl
from jax.experimental.pallas import tpu as pltpu

D_MODEL = 1024
DEPTH = 2
GRID_W = 64
N_Q_HEADS = 8
N_KV_HEADS = 2
GQA_GROUP = N_Q_HEADS // N_KV_HEADS
HEAD_DIM = 128
ATTN_WIDTH = N_Q_HEADS * HEAD_DIM
KV_WIDTH = N_KV_HEADS * HEAD_DIM
ROPE_THETA = 10000.0
ROPE_FREQS = HEAD_DIM // 4
ATTN_SCALE = 1.0 / math.sqrt(HEAD_DIM)
GMLP_WIDTH = D_MODEL
GMLP_GROUPS = 4
GMLP_GROUP_DIM = GMLP_WIDTH // GMLP_GROUPS
CHUNK = 128
CONV_WIDTH = D_MODEL
CONV_KERNEL = 31
CONV_PAD = CONV_KERNEL // 2
D_FF = 4 * D_MODEL
EPS = 1e-6
Q_OFF = 0
K_OFF = Q_OFF + ATTN_WIDTH
V_OFF = K_OFF + KV_WIDTH
GU_OFF = V_OFF + KV_WIDTH
GV_OFF = GU_OFF + GMLP_WIDTH
CG_OFF = GV_OFF + GMLP_WIDTH
GATE_OFF = CG_OFF + 2 * CONV_WIDTH
IN_COLS = GATE_OFF + 3 * D_MODEL

LOG2E = 1.4426950408889634
GELU_C = math.sqrt(2.0 / math.pi)
HALO = 16
MOD_ROWS = 8
VMEM_LIMIT = 56 * 1024 * 1024

F32 = jnp.float32
BF16 = jnp.bfloat16


def _sigmoid(x):
    return 1.0 / (1.0 + jnp.exp(-x))


def _gelu_tanh(x):
    return x * (0.5 * (1.0 + jnp.tanh(GELU_C * (x + 0.044715 * (x * x * x)))))


def _rms(x, g):
    return x * lax.rsqrt(jnp.mean(x * x, axis=-1, keepdims=True) + EPS) * g


def _const_spec(shape):
    n = len(shape)
    return pl.BlockSpec(shape, lambda *_: (0,) * n, pipeline_mode=pl.Buffered(1))


def _mod_kernel(cc_ref, w_ref, b_ref, o_ref):
    cc = cc_ref[...]
    a = cc * _sigmoid(cc)
    o_ref[...] = jnp.dot(a, w_ref[...], preferred_element_type=F32,
                         precision=lax.Precision.HIGHEST) + b_ref[...]


def _mod_call(cc, ada_w, ada_b):
    nblk = 4
    bw = 6 * D_MODEL // nblk
    return pl.pallas_call(
        _mod_kernel,
        out_shape=jax.ShapeDtypeStruct((DEPTH, MOD_ROWS, 6 * D_MODEL), F32),
        grid=(DEPTH, nblk),
        in_specs=[
            pl.BlockSpec((MOD_ROWS, D_MODEL), lambda l, j: (0, 0)),
            pl.BlockSpec((None, D_MODEL, bw), lambda l, j: (l, 0, j)),
            pl.BlockSpec((None, 1, bw), lambda l, j: (l, 0, j)),
        ],
        out_specs=pl.BlockSpec((None, MOD_ROWS, bw), lambda l, j: (l, 0, j)),
        compiler_params=pltpu.CompilerParams(
            dimension_semantics=("parallel", "parallel"), vmem_limit_bytes=VMEM_LIMIT),
        name="adaln_table",
    )(cc, ada_w, ada_b.reshape(DEPTH, 1, 6 * D_MODEL))


def _inproj_kernel(*refs, tm, row, rope):
    if rope:
        (x_ref, mod_ref, pre_g_ref, w_ref, qg_ref, kg_ref, cos_ref, sin_ref, gng_ref, ws_ref, bst_ref,
         wgo_ref, q_ref, k_ref, v_ref, gl_ref, u_ref, ga_ref, gc_ref, h_sc, gp_sc) = refs
    else:
        (x_ref, mod_ref, pre_g_ref, w_ref, qg_ref, kg_ref, gng_ref, ws_ref, bst_ref,
         wgo_ref, q_ref, k_ref, v_ref, gl_ref, u_ref, ga_ref, gc_ref, h_sc, gp_sc) = refs
        cos_ref = sin_ref = None

    sh = mod_ref[row:row + 1, 0:D_MODEL]
    sc = mod_ref[row:row + 1, D_MODEL:2 * D_MODEL]
    h = _rms(x_ref[...], pre_g_ref[...]) * (1.0 + sc) + sh
    h_sc[...] = h.astype(BF16)

    def proj(lo, hi):
        return jnp.dot(h_sc[...], w_ref[:, lo:hi], preferred_element_type=F32)

    def head_norm_rope(t, g):
        t = _rms(t, g)
        if rope:
            t = t * cos_ref[...] + pltpu.roll(t, HEAD_DIM // 2, 1) * sin_ref[...]
        return t

    pq = proj(Q_OFF, K_OFF)
    for hh in range(N_Q_HEADS):
        t = head_norm_rope(pq[:, hh * HEAD_DIM:(hh + 1) * HEAD_DIM], qg_ref[...])
        q_ref[hh] = (t * (ATTN_SCALE * LOG2E)).T.astype(BF16)

    pk = proj(K_OFF, V_OFF)
    for hh in range(N_KV_HEADS):
        k_ref[hh] = head_norm_rope(pk[:, hh * HEAD_DIM:(hh + 1) * HEAD_DIM], kg_ref[...]).astype(BF16)
    pv = proj(V_OFF, GU_OFF)
    for hh in range(N_KV_HEADS):
        v_ref[hh] = pv[:, hh * HEAD_DIM:(hh + 1) * HEAD_DIM].T.astype(BF16)

    gu = _gelu_tanh(proj(GU_OFF, GV_OFF))
    gv = _rms(_gelu_tanh(proj(GV_OFF, CG_OFF)), gng_ref[...]).astype(BF16)
    for c in range(tm // CHUNK):
        rs = slice(c * CHUNK, (c + 1) * CHUNK)
        for g in range(GMLP_GROUPS):
            cs = slice(g * GMLP_GROUP_DIM, (g + 1) * GMLP_GROUP_DIM)
            sv = jnp.dot(ws_ref[g], gv[rs, cs], preferred_element_type=F32) + bst_ref[:, g:g + 1]
            gp_sc[rs, cs] = (gu[rs, cs] * sv).astype(BF16)
    g_out = jnp.dot(gp_sc[...], wgo_ref[...], preferred_element_type=F32)

    pg = proj(GATE_OFF, IN_COLS)
    ga_ref[...] = _sigmoid(pg[:, 0:D_MODEL]).astype(BF16)
    u_ref[...] = (_sigmoid(pg[:, D_MODEL:2 * D_MODEL]) * g_out).astype(BF16)
    gc_ref[...] = _sigmoid(pg[:, 2 * D_MODEL:3 * D_MODEL]).astype(BF16)

    pa = proj(CG_OFF, CG_OFF + CONV_WIDTH)
    pb = proj(CG_OFF + CONV_WIDTH, GATE_OFF)
    gl_ref[...] = (pa * _sigmoid(pb)).astype(BF16)


def _inproj_call(x, mod, pre_g, w_in, qg, kg, cos2, sin2, gng, ws, bst, wgo, *, tm, row):
    n_tok = x.shape[0]
    rope = cos2 is not None
    row_spec = pl.BlockSpec((tm, D_MODEL), lambda i: (i, 0))
    head_tab = pl.BlockSpec((tm, HEAD_DIM), lambda i: (i, 0))
    in_specs = [row_spec, _const_spec((MOD_ROWS, 6 * D_MODEL)), _const_spec((1, D_MODEL)),
                _const_spec((D_MODEL, IN_COLS)), _const_spec((1, HEAD_DIM)), _const_spec((1, HEAD_DIM))]
    args = [x, mod, pre_g, w_in, qg, kg]
    if rope:
        in_specs += [head_tab, head_tab]
        args += [cos2, sin2]
    in_specs += [_const_spec((1, D_MODEL)), _const_spec((GMLP_GROUPS, CHUNK, CHUNK)),
                 _const_spec((CHUNK, GMLP_GROUPS)), _const_spec((D_MODEL, D_MODEL))]
    args += [gng, ws, bst, wgo]

    def heads(n):
        return (jax.ShapeDtypeStruct((n, n_tok, HEAD_DIM), BF16),
                pl.BlockSpec((n, tm, HEAD_DIM), lambda i: (0, i, 0)))

    def heads_t(n):
        return (jax.ShapeDtypeStruct((n, HEAD_DIM, n_tok), BF16),
                pl.BlockSpec((n, HEAD_DIM, tm), lambda i: (0, 0, i)))

    wide = (jax.ShapeDtypeStruct((n_tok, D_MODEL), BF16), row_spec)
    outs = [heads_t(N_Q_HEADS), heads(N_KV_HEADS), heads_t(N_KV_HEADS), wide, wide, wide, wide]
    return pl.pallas_call(
        functools.partial(_inproj_kernel, tm=tm, row=row, rope=rope),
        out_shape=[o[0] for o in outs],
        grid=(n_tok // tm,),
        in_specs=in_specs,
        out_specs=[o[1] for o in outs],
        scratch_shapes=[pltpu.VMEM((tm, D_MODEL), BF16), pltpu.VMEM((tm, D_MODEL), BF16)],
        compiler_params=pltpu.CompilerParams(
            dimension_semantics=("parallel",), vmem_limit_bytes=VMEM_LIMIT),
        name="inproj_rope" if rope else "inproj_ctx",
    )(*args)


def _flash_kernel(*refs, tq, tk, n_chunks):
    if n_chunks:
        qt_ref, kc_ref, vct_ref, k_ref, vt_ref, o_ref, m_sc, l_sc, acc_sc, st_sc = refs
    else:
        qt_ref, kc_ref, vct_ref, o_ref, m_sc, l_sc, acc_sc = refs

    def block(g, k, vt, first, slot=0):
        if first:
            st = jnp.dot(k, qt_ref[g], preferred_element_type=F32)
        else:
            st = st_sc[slot, g]
        m_cur = jnp.max(st, axis=0, keepdims=True)
        if first:
            m_new = m_cur
        else:
            m_prev = m_sc[g]
            m_new = jnp.maximum(m_prev, m_cur)
            alpha = jnp.exp2(m_prev - m_new)
        pt = jnp.exp2(st - m_new)
        l_cur = jnp.sum(pt, axis=0, keepdims=True)
        pv = jnp.dot(vt, pt.astype(BF16), preferred_element_type=F32)
        if first:
            l_sc[g] = l_cur
            acc_sc[g] = pv
        else:
            l_sc[g] = alpha * l_sc[g] + l_cur
            acc_sc[g] = alpha * acc_sc[g] + pv
        m_sc[g] = m_new

    for g in range(GQA_GROUP):
        block(g, kc_ref[...], vct_ref[...], True)

    if n_chunks:
        for g in range(GQA_GROUP):
            st_sc[0, g] = jnp.dot(k_ref[pl.ds(0, tk), :], qt_ref[g], preferred_element_type=F32)

        def stage(c, slot):
            off = pl.multiple_of(c * tk, tk)
            off_next = pl.multiple_of(jnp.minimum(c + 1, n_chunks - 1) * tk, tk)
            vt = vt_ref[:, pl.ds(off, tk)]
            k_next = k_ref[pl.ds(off_next, tk), :]
            for g in range(GQA_GROUP):
                st_sc[1 - slot, g] = jnp.dot(k_next, qt_ref[g], preferred_element_type=F32)
                block(g, None, vt, False, slot)

        def body(j, carry):
            stage(2 * j, 0)
            stage(2 * j + 1, 1)
            return carry

        lax.fori_loop(0, n_chunks // 2, body, 0)

    for g in range(GQA_GROUP):
        o = acc_sc[g] / l_sc[g]
        o_ref[:, g * HEAD_DIM:(g + 1) * HEAD_DIM] = o.T.astype(BF16)


def _flash_call(qt, kc, vct, k, vt, *, tq, tk):
    lq = qt.shape[2]
    lc = kc.shape[1]
    n_chunks = 0 if k is None else k.shape[1] // tk
    assert n_chunks % 2 == 0
    in_specs = [pl.BlockSpec((GQA_GROUP, HEAD_DIM, tq), lambda h, i: (h, 0, i)),
                pl.BlockSpec((None, lc, HEAD_DIM), lambda h, i: (h, 0, 0)),
                pl.BlockSpec((None, HEAD_DIM, lc), lambda h, i: (h, 0, 0))]
    args = [qt, kc, vct]
    if n_chunks:
        s_len = k.shape[1]
        in_specs += [pl.BlockSpec((None, s_len, HEAD_DIM), lambda h, i: (h, 0, 0)),
                     pl.BlockSpec((None, HEAD_DIM, s_len), lambda h, i: (h, 0, 0))]
        args += [k, vt]
    return pl.pallas_call(
        functools.partial(_flash_kernel, tq=tq, tk=tk, n_chunks=n_chunks),
        out_shape=jax.ShapeDtypeStruct((lq, ATTN_WIDTH), BF16),
        grid=(N_KV_HEADS, lq // tq),
        in_specs=in_specs,
        out_specs=pl.BlockSpec((tq, GQA_GROUP * HEAD_DIM), lambda h, i: (i, h)),
        scratch_shapes=[pltpu.VMEM((GQA_GROUP, 1, tq), F32), pltpu.VMEM((GQA_GROUP, 1, tq), F32),
                        pltpu.VMEM((GQA_GROUP, HEAD_DIM, tq), F32)]
        + ([pltpu.VMEM((2, GQA_GROUP, tk, tq), F32)] if n_chunks else []),
        compiler_params=pltpu.CompilerParams(
            dimension_semantics=("parallel", "parallel"), vmem_limit_bytes=VMEM_LIMIT),
        name="flash_latent" if n_chunks else "flash_ctx",
    )(*args)


def _post_kernel(x_ref, attn_ref, glp_ref, gl_ref, gln_ref, u_ref, ga_ref, gc_ref, mod_ref,
                 post_g_ref, fpre_g_ref, fpost_g_ref, cw_ref, cb_ref, lng_ref, lnb_ref,
                 wao_ref, wco_ref, wout_ref, w1_ref, w2_ref, xo_ref, win_sc, conv_sc, *, tm, row):
    i = pl.program_id(0)
    n = pl.num_programs(0)

    keep_prev = jnp.where(i > 0, 1.0, 0.0).astype(F32)
    keep_next = jnp.where(i < n - 1, 1.0, 0.0).astype(F32)
    win_sc[0:HALO, :] = glp_ref[...].astype(F32) * keep_prev
    win_sc[HALO:HALO + tm, :] = gl_ref[...].astype(F32)
    win_sc[HALO + tm:HALO + tm + HALO, :] = gln_ref[...].astype(F32) * keep_next
    rb = 32
    for r in range(tm // rb):
        for cblk in range(D_MODEL // 128):
            cs = slice(cblk * 128, (cblk + 1) * 128)
            acc = jnp.broadcast_to(cb_ref[:, cs], (rb, 128))
            for kk in range(CONV_KERNEL):
                start = r * rb + kk + HALO - CONV_PAD
                acc = acc + cw_ref[kk:kk + 1, cs] * win_sc[start:start + rb, cs]
            conv_sc[r * rb:(r + 1) * rb, cs] = acc
    y = conv_sc[...]
    mu = jnp.mean(y, axis=-1, keepdims=True)
    yc = y - mu
    var = jnp.mean(yc * yc, axis=-1, keepdims=True)
    y = yc * lax.rsqrt(var + EPS) * lng_ref[...] + lnb_ref[...]
    y = y * _sigmoid(y)
    c_out = jnp.dot(y.astype(BF16), wco_ref[...], preferred_element_type=F32)

    a_out = jnp.dot(attn_ref[...], wao_ref[...], preferred_element_type=F32)
    merged = (ga_ref[...].astype(F32) * a_out + u_ref[...].astype(F32)
              + gc_ref[...].astype(F32) * c_out)
    ymix = jnp.dot(merged.astype(BF16), wout_ref[...], preferred_element_type=F32)

    def mod(j):
        return mod_ref[row:row + 1, j * D_MODEL:(j + 1) * D_MODEL]

    x1 = x_ref[...] + mod(2) * _rms(ymix, post_g_ref[...])
    h2 = (_rms(x1, fpre_g_ref[...]) * (1.0 + mod(4)) + mod(3)).astype(BF16)
    f = jnp.dot(h2, w1_ref[...], preferred_element_type=F32)
    f = jnp.maximum(f, 0.0)
    f = (f * f).astype(BF16)
    f2 = jnp.dot(f, w2_ref[...], preferred_element_type=F32)
    xo_ref[...] = x1 + mod(5) * _rms(f2, fpost_g_ref[...])


def _post_call(x, attn, gl, u, ga, gc, mod, post_g, fpre_g, fpost_g, cw, cb, lng, lnb,
               wao, wco, wout, w1, w2, *, tm, row):
    n_tok = x.shape[0]
    hb = tm // HALO
    n_hb = n_tok // HALO
    row_spec = pl.BlockSpec((tm, D_MODEL), lambda i: (i, 0))
    prev_spec = pl.BlockSpec((HALO, D_MODEL), lambda i: (jnp.maximum(i * hb - 1, 0), 0))
    next_spec = pl.BlockSpec((HALO, D_MODEL), lambda i: (jnp.minimum((i + 1) * hb, n_hb - 1), 0))
    vec = _const_spec((1, D_MODEL))
    sq = _const_spec((D_MODEL, D_MODEL))
    in_specs = [row_spec, row_spec, prev_spec, row_spec, next_spec, row_spec, row_spec, row_spec,
                _const_spec((MOD_ROWS, 6 * D_MODEL)), vec, vec, vec,
                _const_spec((CONV_KERNEL, D_MODEL)), vec, vec, vec,
                sq, sq, sq, _const_spec((D_MODEL, D_FF)), _const_spec((D_FF, D_MODEL))]
    return pl.pallas_call(
        functools.partial(_post_kernel, tm=tm, row=row),
        out_shape=jax.ShapeDtypeStruct((n_tok, D_MODEL), F32),
        grid=(n_tok // tm,),
        in_specs=in_specs,
        out_specs=row_spec,
        scratch_shapes=[pltpu.VMEM((tm + 2 * HALO, D_MODEL), F32), pltpu.VMEM((tm, D_MODEL), F32)],
        compiler_params=pltpu.CompilerParams(
            dimension_semantics=("parallel",), vmem_limit_bytes=VMEM_LIMIT),
        name="merge_mlp",
    )(x, attn, gl, gl, gl, u, ga, gc, mod, post_g, fpre_g, fpost_g, cw, cb, lng, lnb,
      wao, wco, wout, w1, w2)


def _rope_tables(n_rows):
    row = jnp.repeat(jnp.arange(n_rows), GRID_W).astype(F32)
    col = jnp.tile(jnp.arange(GRID_W), n_rows).astype(F32)
    inv = ROPE_THETA ** (-jnp.arange(ROPE_FREQS, dtype=F32) / ROPE_FREQS)
    ang = jnp.concatenate([row[:, None] * inv, col[:, None] * inv], axis=-1)
    cos, sin = jnp.cos(ang), jnp.sin(ang)
    return jnp.concatenate([cos, cos], axis=-1), jnp.concatenate([-sin, sin], axis=-1)


def kernel(x, c, ctx, c_ctx, ada_w, ada_b, mix_pre_g, mix_post_g, w_in, q_norm_g, k_norm_g, w_attn_o,
           gmlp_norm_g, gmlp_ws, gmlp_bs, w_gmlp_o, conv_w, conv_b, conv_norm_g, conv_norm_b, w_conv_o,
           w_out, ffn_pre_g, ffn_post_g, w_ff1, w_ff2):
    batch, seq, _ = x.shape
    assert batch == 1 and c.shape[0] == 1 and ctx.shape[0] == 1
    n_ctx = ctx.shape[1]
    tm_lat, tm_ctx = 256, n_ctx
    tq, tk = 512, 512
    assert seq % GRID_W == 0 and seq % tm_lat == 0 and seq % tk == 0 and seq % tq == 0 and n_ctx % CHUNK == 0

    xl, xc = x[0], ctx[0]
    cc = jnp.zeros((MOD_ROWS, D_MODEL), F32).at[0].set(c[0]).at[1].set(c_ctx)
    mods = _mod_call(cc, ada_w, ada_b)
    cos2, sin2 = _rope_tables(seq // GRID_W)

    def vec(a):
        return a.reshape(1, -1)

    for l in range(DEPTH):
        last = l == DEPTH - 1
        mod = mods[l]
        inproj_w = (vec(mix_pre_g[l]), w_in[l].astype(BF16), vec(q_norm_g[l]), vec(k_norm_g[l]))
        gmlp_w = (vec(gmlp_norm_g[l]), gmlp_ws[l].astype(BF16), gmlp_bs[l].T, w_gmlp_o[l].astype(BF16))
        post_w = (vec(mix_post_g[l]), vec(ffn_pre_g[l]), vec(ffn_post_g[l]), conv_w[l], vec(conv_b[l]),
                  vec(conv_norm_g[l]), vec(conv_norm_b[l]), w_attn_o[l].astype(BF16),
                  w_conv_o[l].astype(BF16), w_out[l].astype(BF16), w_ff1[l].astype(BF16),
                  w_ff2[l].astype(BF16))

        qt, k, vt, gl, u, ga, gc = _inproj_call(xl, mod, *inproj_w, cos2, sin2, *gmlp_w, tm=tm_lat, row=0)
        qct, kc, vct, glc, uc, gac, gcc = _inproj_call(xc, mod, *inproj_w, None, None, *gmlp_w,
                                                       tm=tm_ctx, row=1)
        attn = _flash_call(qt, kc, vct, k, vt, tq=tq, tk=tk)
        if not last:
            attn_c = _flash_call(qct, kc, vct, None, None, tq=n_ctx, tk=tk)
            xc = _post_call(xc, attn_c, glc, uc, gac, gcc, mod, *post_w, tm=tm_ctx, row=1)
        xl = _post_call(xl, attn, gl, u, ga, gc, mod, *post_w, tm=tm_lat, row=0)
    return xl[None]
```

```python
import functools
import math

import jax
import jax.numpy as jnp
from jax import lax
from jax.experimental import pallas as pl
from jax.experimental.pallas import tpu as pltpu

D_MODEL = 1024
DEPTH = 2
GRID_W = 64
N_Q_HEADS = 8
N_KV_HEADS = 2
GQA_GROUP = N_Q_HEADS // N_KV_HEADS
HEAD_DIM = 128
ATTN_WIDTH = N_Q_HEADS * HEAD_DIM
KV_WIDTH = N_KV_HEADS * HEAD_DIM
ROPE_THETA = 10000.0
ROPE_FREQS = HEAD_DIM // 4
ATTN_SCALE = 1.0 / math.sqrt(HEAD_DIM)
GMLP_WIDTH = D_MODEL
GMLP_GROUPS = 4
GMLP_GROUP_DIM = GMLP_WIDTH // GMLP_GROUPS
CHUNK = 128
CONV_WIDTH = D_MODEL
CONV_KERNEL = 31
CONV_PAD = CONV_KERNEL // 2
D_FF = 4 * D_MODEL
EPS = 1e-6
Q_OFF = 0
K_OFF = Q_OFF + ATTN_WIDTH
V_OFF = K_OFF + KV_WIDTH
GU_OFF = V_OFF + KV_WIDTH
GV_OFF = GU_OFF + GMLP_WIDTH
CG_OFF = GV_OFF + GMLP_WIDTH
GATE_OFF = CG_OFF + 2 * CONV_WIDTH
IN_COLS = GATE_OFF + 3 * D_MODEL

LOG2E = 1.4426950408889634
GELU_C = math.sqrt(2.0 / math.pi)
LANES = 128
SUBLANES = 8
HALO = 16
SHIFT_SPAN_EXTRA = SUBLANES * (CONV_KERNEL // SUBLANES)
SUM_ROWS = 16
V_ROWS = HEAD_DIM + SUM_ROWS
MOD_ROWS = 8
VMEM_LIMIT = 56 * 1024 * 1024

F32 = jnp.float32
BF16 = jnp.bfloat16


def _sigmoid(x):
    return 1.0 / (1.0 + jnp.exp(-x))


def _gelu_tanh(x):
    return x * (0.5 * (1.0 + jnp.tanh(GELU_C * (x + 0.044715 * (x * x * x)))))


def _rms(x, g):
    return x * lax.rsqrt(jnp.mean(x * x, axis=-1, keepdims=True) + EPS) * g


def _const_spec(shape):
    n = len(shape)
    return pl.BlockSpec(shape, lambda *_: (0,) * n, pipeline_mode=pl.Buffered(1))


def _mod_kernel(cc_ref, w_ref, b_ref, o_ref):
    cc = cc_ref[...]
    a = cc * _sigmoid(cc)
    o_ref[...] = jnp.dot(a, w_ref[...], preferred_element_type=F32,
                         precision=lax.Precision.HIGHEST) + b_ref[...]


def _mod_call(cc, ada_w, ada_b):
    nblk = 4
    bw = 6 * D_MODEL // nblk
    return pl.pallas_call(
        _mod_kernel,
        out_shape=jax.ShapeDtypeStruct((DEPTH, MOD_ROWS, 6 * D_MODEL), F32),
        grid=(DEPTH, nblk),
        in_specs=[
            pl.BlockSpec((MOD_ROWS, D_MODEL), lambda l, j: (0, 0)),
            pl.BlockSpec((None, D_MODEL, bw), lambda l, j: (l, 0, j)),
            pl.BlockSpec((None, 1, bw), lambda l, j: (l, 0, j)),
        ],
        out_specs=pl.BlockSpec((None, MOD_ROWS, bw), lambda l, j: (l, 0, j)),
        compiler_params=pltpu.CompilerParams(
            dimension_semantics=("parallel", "parallel"), vmem_limit_bytes=VMEM_LIMIT),
        name="adaln_table",
    )(cc, ada_w, ada_b.reshape(DEPTH, 1, 6 * D_MODEL))


def _inproj_kernel(*refs, tm, row, rope):
    if rope:
        (x_ref, mod_ref, pre_g_ref, w_ref, qg_ref, kg_ref, cos_ref, sin_ref, gng_ref, ws_ref, bst_ref,
         wgo_ref, q_ref, k_ref, v_ref, gl_ref, u_ref, ga_ref, gc_ref, h_sc, gp_sc) = refs
    else:
        (x_ref, mod_ref, pre_g_ref, w_ref, qg_ref, kg_ref, gng_ref, ws_ref, bst_ref,
         wgo_ref, q_ref, k_ref, v_ref, gl_ref, u_ref, ga_ref, gc_ref, h_sc, gp_sc) = refs
        cos_ref = sin_ref = None

    sh = mod_ref[row:row + 1, 0:D_MODEL]
    sc = mod_ref[row:row + 1, D_MODEL:2 * D_MODEL]
    h = _rms(x_ref[...], pre_g_ref[...]) * (1.0 + sc) + sh
    h_sc[...] = h.astype(BF16)

    def proj(lo, hi):
        return jnp.dot(h_sc[...], w_ref[:, lo:hi], preferred_element_type=F32)

    def head_norm_rope(t, g):
        t = _rms(t, g)
        if rope:
            t = t * cos_ref[...] + pltpu.roll(t, HEAD_DIM // 2, 1) * sin_ref[...]
        return t

    pq = proj(Q_OFF, K_OFF)
    for hh in range(N_Q_HEADS):
        t = head_norm_rope(pq[:, hh * HEAD_DIM:(hh + 1) * HEAD_DIM], qg_ref[...])
        q_ref[hh] = (t * (ATTN_SCALE * LOG2E)).T.astype(BF16)

    pk = proj(K_OFF, V_OFF)
    for hh in range(N_KV_HEADS):
        k_ref[hh] = head_norm_rope(pk[:, hh * HEAD_DIM:(hh + 1) * HEAD_DIM], kg_ref[...]).astype(BF16)
    pv = proj(V_OFF, GU_OFF)
    for hh in range(N_KV_HEADS):
        v_ref[hh, 0:HEAD_DIM, :] = pv[:, hh * HEAD_DIM:(hh + 1) * HEAD_DIM].T.astype(BF16)
        v_ref[hh, HEAD_DIM:HEAD_DIM + SUM_ROWS, :] = jnp.ones((SUM_ROWS, tm), BF16)

    gu = _gelu_tanh(proj(GU_OFF, GV_OFF))
    gv = _rms(_gelu_tanh(proj(GV_OFF, CG_OFF)), gng_ref[...]).astype(BF16)
    for c in range(tm // CHUNK):
        rs = slice(c * CHUNK, (c + 1) * CHUNK)
        for g in range(GMLP_GROUPS):
            cs = slice(g * GMLP_GROUP_DIM, (g + 1) * GMLP_GROUP_DIM)
            sv = jnp.dot(ws_ref[g], gv[rs, cs], preferred_element_type=F32) + bst_ref[:, g:g + 1]
            gp_sc[rs, cs] = (gu[rs, cs] * sv).astype(BF16)
    g_out = jnp.dot(gp_sc[...], wgo_ref[...], preferred_element_type=F32)

    pg = proj(GATE_OFF, IN_COLS)
    ga_ref[...] = _sigmoid(pg[:, 0:D_MODEL]).astype(BF16)
    u_ref[...] = (_sigmoid(pg[:, D_MODEL:2 * D_MODEL]) * g_out).astype(BF16)
    gc_ref[...] = _sigmoid(pg[:, 2 * D_MODEL:3 * D_MODEL]).astype(BF16)

    pa = proj(CG_OFF, CG_OFF + CONV_WIDTH)
    pb = proj(CG_OFF + CONV_WIDTH, GATE_OFF)
    gl_ref[...] = (pa * _sigmoid(pb)).astype(BF16)


def _inproj_call(x, mod, pre_g, w_in, qg, kg, cos2, sin2, gng, ws, bst, wgo, *, tm, row):
    n_tok = x.shape[0]
    rope = cos2 is not None
    row_spec = pl.BlockSpec((tm, D_MODEL), lambda i: (i, 0))
    head_tab = pl.BlockSpec((tm, HEAD_DIM), lambda i: (i, 0))
    in_specs = [row_spec, _const_spec((MOD_ROWS, 6 * D_MODEL)), _const_spec((1, D_MODEL)),
                _const_spec((D_MODEL, IN_COLS)), _const_spec((1, HEAD_DIM)), _const_spec((1, HEAD_DIM))]
    args = [x, mod, pre_g, w_in, qg, kg]
    if rope:
        in_specs += [head_tab, head_tab]
        args += [cos2, sin2]
    in_specs += [_const_spec((1, D_MODEL)), _const_spec((GMLP_GROUPS, CHUNK, CHUNK)),
                 _const_spec((CHUNK, GMLP_GROUPS)), _const_spec((D_MODEL, D_MODEL))]
    args += [gng, ws, bst, wgo]

    def heads(n):
        return (jax.ShapeDtypeStruct((n, n_tok, HEAD_DIM), BF16),
                pl.BlockSpec((n, tm, HEAD_DIM), lambda i: (0, i, 0)))

    def heads_t(n, rows):
        return (jax.ShapeDtypeStruct((n, rows, n_tok), BF16),
                pl.BlockSpec((n, rows, tm), lambda i: (0, 0, i)))

    wide = (jax.ShapeDtypeStruct((n_tok, D_MODEL), BF16), row_spec)
    outs = [heads_t(N_Q_HEADS, HEAD_DIM), heads(N_KV_HEADS), heads_t(N_KV_HEADS, V_ROWS),
            wide, wide, wide, wide]
    return pl.pallas_call(
        functools.partial(_inproj_kernel, tm=tm, row=row, rope=rope),
        out_shape=[o[0] for o in outs],
        grid=(n_tok // tm,),
        in_specs=in_specs,
        out_specs=[o[1] for o in outs],
        scratch_shapes=[pltpu.VMEM((tm, D_MODEL), BF16), pltpu.VMEM((tm, D_MODEL), BF16)],
        compiler_params=pltpu.CompilerParams(
            dimension_semantics=("parallel",), vmem_limit_bytes=VMEM_LIMIT),
        name="inproj_rope" if rope else "inproj_ctx",
    )(*args)


def _flash_kernel(*refs, tq, tk, n_chunks):
    if n_chunks:
        qt_ref, kc_ref, vct_ref, k_ref, vt_ref, o_ref, m_sc, acc_sc, st_sc = refs
    else:
        qt_ref, kc_ref, vct_ref, o_ref, m_sc, acc_sc = refs

    def block(g, k, vt, first, slot=0):
        if first:
            st = jnp.dot(k, qt_ref[g], preferred_element_type=F32)
        else:
            st = st_sc[slot, g]
        m_cur = jnp.max(st, axis=0, keepdims=True)
        if first:
            m_new = m_cur
        else:
            m_prev = m_sc[g]
            m_new = jnp.maximum(m_prev, m_cur)
            alpha = jnp.exp2(m_prev - m_new)
        pt = jnp.exp2(st - m_new)
        pv = jnp.dot(vt, pt.astype(BF16), preferred_element_type=F32)
        if first:
            acc_sc[g] = pv
        else:
            acc_sc[g] = alpha * acc_sc[g] + pv
        m_sc[g] = m_new

    for g in range(GQA_GROUP):
        block(g, kc_ref[...], vct_ref[...], True)

    if n_chunks:
        for g in range(GQA_GROUP):
            st_sc[0, g] = jnp.dot(k_ref[pl.ds(0, tk), :], qt_ref[g], preferred_element_type=F32)

        def stage(c, slot):
            off = pl.multiple_of(c * tk, tk)
            off_next = pl.multiple_of(jnp.minimum(c + 1, n_chunks - 1) * tk, tk)
            vt = vt_ref[:, pl.ds(off, tk)]
            k_next = k_ref[pl.ds(off_next, tk), :]
            for g in range(GQA_GROUP):
                st_sc[1 - slot, g] = jnp.dot(k_next, qt_ref[g], preferred_element_type=F32)
                block(g, None, vt, False, slot)

        def body(j, carry):
            stage(2 * j, 0)
            stage(2 * j + 1, 1)
            return carry

        lax.fori_loop(0, n_chunks // 2, body, 0)

    for g in range(GQA_GROUP):
        o = acc_sc[g, 0:HEAD_DIM, :] / acc_sc[g, HEAD_DIM:HEAD_DIM + 1, :]
        o_ref[:, g * HEAD_DIM:(g + 1) * HEAD_DIM] = o.T.astype(BF16)


def _flash_call(qt, kc, vct, k, vt, *, tq, tk):
    lq = qt.shape[2]
    lc = kc.shape[1]
    n_chunks = 0 if k is None else k.shape[1] // tk
    assert n_chunks % 2 == 0
    in_specs = [pl.BlockSpec((GQA_GROUP, HEAD_DIM, tq), lambda h, i: (h, 0, i)),
                pl.BlockSpec((None, lc, HEAD_DIM), lambda h, i: (h, 0, 0)),
                pl.BlockSpec((None, V_ROWS, lc), lambda h, i: (h, 0, 0))]
    args = [qt, kc, vct]
    if n_chunks:
        s_len = k.shape[1]
        in_specs += [pl.BlockSpec((None, s_len, HEAD_DIM), lambda h, i: (h, 0, 0)),
                     pl.BlockSpec((None, V_ROWS, s_len), lambda h, i: (h, 0, 0))]
        args += [k, vt]
    return pl.pallas_call(
        functools.partial(_flash_kernel, tq=tq, tk=tk, n_chunks=n_chunks),
        out_shape=jax.ShapeDtypeStruct((lq, ATTN_WIDTH), BF16),
        grid=(N_KV_HEADS, lq // tq),
        in_specs=in_specs,
        out_specs=pl.BlockSpec((tq, GQA_GROUP * HEAD_DIM), lambda h, i: (i, h)),
        scratch_shapes=[pltpu.VMEM((GQA_GROUP, 1, tq), F32), pltpu.VMEM((GQA_GROUP, V_ROWS, tq), F32)]
        + ([pltpu.VMEM((2, GQA_GROUP, tk, tq), F32)] if n_chunks else []),
        compiler_params=pltpu.CompilerParams(
            dimension_semantics=("parallel", "parallel"), vmem_limit_bytes=VMEM_LIMIT),
        name="flash_latent" if n_chunks else "flash_ctx",
    )(*args)


def _post_kernel(x_ref, attn_ref, glp_ref, gl_ref, gln_ref, u_ref, ga_ref, gc_ref, mod_ref,
                 post_g_ref, fpre_g_ref, fpost_g_ref, cw_ref, cb_ref, lng_ref, lnb_ref,
                 wao_ref, wco_ref, wout_ref, w1_ref, w2_ref, xo_ref, win_sc, shift_sc, conv_sc, *, tm, row):
    i = pl.program_id(0)
    n = pl.num_programs(0)
    span = tm + SHIFT_SPAN_EXTRA
    rb = 32

    keep_prev = jnp.where(i > 0, 1.0, 0.0).astype(F32)
    keep_next = jnp.where(i < n - 1, 1.0, 0.0).astype(F32)
    win_sc[0:HALO, :] = glp_ref[...].astype(F32) * keep_prev
    win_sc[HALO:HALO + tm, :] = gl_ref[...].astype(F32)
    win_sc[HALO + tm:HALO + tm + HALO, :] = gln_ref[...].astype(F32) * keep_next
    for cblk in range(D_MODEL // LANES):
        cs = slice(cblk * LANES, (cblk + 1) * LANES)
        buf = cblk % 2
        for s in range(1, SUBLANES):
            shift_sc[buf, s - 1] = win_sc[s:s + span, cs]
        for r in range(tm // rb):
            acc = jnp.broadcast_to(cb_ref[:, cs], (rb, LANES))
            for d in range(1, CONV_KERNEL + 1):
                a, s = divmod(d, SUBLANES)
                base = r * rb + SUBLANES * a
                if s == 0:
                    src = win_sc[base:base + rb, cs]
                else:
                    src = shift_sc[buf, s - 1, base:base + rb, :]
                acc = acc + cw_ref[d - 1:d, cs] * src
            conv_sc[r * rb:(r + 1) * rb, cs] = acc
    y = conv_sc[...]
    mu = jnp.mean(y, axis=-1, keepdims=True)
    yc = y - mu
    var = jnp.mean(yc * yc, axis=-1, keepdims=True)
    y = yc * lax.rsqrt(var + EPS) * lng_ref[...] + lnb_ref[...]
    y = y * _sigmoid(y)
    c_out = jnp.dot(y.astype(BF16), wco_ref[...], preferred_element_type=F32)

    a_out = jnp.dot(attn_ref[...], wao_ref[...], preferred_element_type=F32)
    merged = (ga_ref[...].astype(F32) * a_out + u_ref[...].astype(F32)
              + gc_ref[...].astype(F32) * c_out)
    ymix = jnp.dot(merged.astype(BF16), wout_ref[...], preferred_element_type=F32)

    def mod(j):
        return mod_ref[row:row + 1, j * D_MODEL:(j + 1) * D_MODEL]

    x1 = x_ref[...] + mod(2) * _rms(ymix, post_g_ref[...])
    h2 = (_rms(x1, fpre_g_ref[...]) * (1.0 + mod(4)) + mod(3)).astype(BF16)
    f = jnp.dot(h2, w1_ref[...], preferred_element_type=F32)
    f = jnp.maximum(f, 0.0)
    f = (f * f).astype(BF16)
    f2 = jnp.dot(f, w2_ref[...], preferred_element_type=F32)
    xo_ref[...] = x1 + mod(5) * _rms(f2, fpost_g_ref[...])


def _post_call(x, attn, gl, u, ga, gc, mod, post_g, fpre_g, fpost_g, cw, cb, lng, lnb,
               wao, wco, wout, w1, w2, *, tm, row):
    n_tok = x.shape[0]
    hb = tm // HALO
    n_hb = n_tok // HALO
    row_spec = pl.BlockSpec((tm, D_MODEL), lambda i: (i, 0))
    prev_spec = pl.BlockSpec((HALO, D_MODEL), lambda i: (jnp.maximum(i * hb - 1, 0), 0))
    next_spec = pl.BlockSpec((HALO, D_MODEL), lambda i: (jnp.minimum((i + 1) * hb, n_hb - 1), 0))
    vec = _const_spec((1, D_MODEL))
    sq = _const_spec((D_MODEL, D_MODEL))
    in_specs = [row_spec, row_spec, prev_spec, row_spec, next_spec, row_spec, row_spec, row_spec,
                _const_spec((MOD_ROWS, 6 * D_MODEL)), vec, vec, vec,
                _const_spec((CONV_KERNEL, D_MODEL)), vec, vec, vec,
                sq, sq, sq, _const_spec((D_MODEL, D_FF)), _const_spec((D_FF, D_MODEL))]
    return pl.pallas_call(
        functools.partial(_post_kernel, tm=tm, row=row),
        out_shape=jax.ShapeDtypeStruct((n_tok, D_MODEL), F32),
        grid=(n_tok // tm,),
        in_specs=in_specs,
        out_specs=row_spec,
        scratch_shapes=[pltpu.VMEM((tm + 2 * HALO, D_MODEL), F32),
                        pltpu.VMEM((2, SUBLANES - 1, tm + SHIFT_SPAN_EXTRA, LANES), F32),
                        pltpu.VMEM((tm, D_MODEL), F32)],
        compiler_params=pltpu.CompilerParams(
            dimension_semantics=("parallel",), vmem_limit_bytes=VMEM_LIMIT),
        name="merge_mlp",
    )(x, attn, gl, gl, gl, u, ga, gc, mod, post_g, fpre_g, fpost_g, cw, cb, lng, lnb,
      wao, wco, wout, w1, w2)


def _rope_tables(n_rows):
    row = jnp.repeat(jnp.arange(n_rows), GRID_W).astype(F32)
    col = jnp.tile(jnp.arange(GRID_W), n_rows).astype(F32)
    inv = ROPE_THETA ** (-jnp.arange(ROPE_FREQS, dtype=F32) / ROPE_FREQS)
    ang = jnp.concatenate([row[:, None] * inv, col[:, None] * inv], axis=-1)
    cos, sin = jnp.cos(ang), jnp.sin(ang)
    return jnp.concatenate([cos, cos], axis=-1), jnp.concatenate([-sin, sin], axis=-1)


def kernel(x, c, ctx, c_ctx, ada_w, ada_b, mix_pre_g, mix_post_g, w_in, q_norm_g, k_norm_g, w_attn_o,
           gmlp_norm_g, gmlp_ws, gmlp_bs, w_gmlp_o, conv_w, conv_b, conv_norm_g, conv_norm_b, w_conv_o,
           w_out, ffn_pre_g, ffn_post_g, w_ff1, w_ff2):
    batch, seq, _ = x.shape
    assert batch == 1 and c.shape[0] == 1 and ctx.shape[0] == 1
    n_ctx = ctx.shape[1]
    tm_lat, tm_ctx = 256, n_ctx
    tq, tk = 512, 512
    assert seq % GRID_W == 0 and seq % tm_lat == 0 and seq % tk == 0 and seq % tq == 0 and n_ctx % CHUNK == 0

    xl, xc = x[0], ctx[0]
    cc = jnp.zeros((MOD_ROWS, D_MODEL), F32).at[0].set(c[0]).at[1].set(c_ctx)
    mods = _mod_call(cc, ada_w, ada_b)
    cos2, sin2 = _rope_tables(seq // GRID_W)

    def vec(a):
        return a.reshape(1, -1)

    for l in range(DEPTH):
        last = l == DEPTH - 1
        mod = mods[l]
        inproj_w = (vec(mix_pre_g[l]), w_in[l].astype(BF16), vec(q_norm_g[l]), vec(k_norm_g[l]))
        gmlp_w = (vec(gmlp_norm_g[l]), gmlp_ws[l].astype(BF16), gmlp_bs[l].T, w_gmlp_o[l].astype(BF16))
        post_w = (vec(mix_post_g[l]), vec(ffn_pre_g[l]), vec(ffn_post_g[l]), conv_w[l], vec(conv_b[l]),
                  vec(conv_norm_g[l]), vec(conv_norm_b[l]), w_attn_o[l].astype(BF16),
                  w_conv_o[l].astype(BF16), w_out[l].astype(BF16), w_ff1[l].astype(BF16),
                  w_ff2[l].astype(BF16))

        qt, k, vt, gl, u, ga, gc = _inproj_call(xl, mod, *inproj_w, cos2, sin2, *gmlp_w, tm=tm_lat, row=0)
        qct, kc, vct, glc, uc, gac, gcc = _inproj_call(xc, mod, *inproj_w, None, None, *gmlp_w,
                                                       tm=tm_ctx, row=1)
        attn = _flash_call(qt, kc, vct, k, vt, tq=tq, tk=tk)
        if not last:
            attn_c = _flash_call(qct, kc, vct, None, None, tq=n_ctx, tk=tk)
            xc = _post_call(xc, attn_c, glc, uc, gac, gcc, mod, *post_w, tm=tm_ctx, row=1)
        xl = _post_call(xl, attn, gl, u, ga, gc, mod, *post_w, tm=tm_lat, row=0)
    return xl[None]
```

```python
import functools
import math

import jax
import jax.numpy as jnp
from jax import lax
from jax.experimental import pallas as pl
from jax.experimental.pallas import tpu as pltpu

D_MODEL = 1024
DEPTH = 2
GRID_W = 64
N_Q_HEADS = 8
N_KV_HEADS = 2
GQA_GROUP = N_Q_HEADS // N_KV_HEADS
HEAD_DIM = 128
ATTN_WIDTH = N_Q_HEADS * HEAD_DIM
KV_WIDTH = N_KV_HEADS * HEAD_DIM
ROPE_THETA = 10000.0
ROPE_FREQS = HEAD_DIM // 4
ATTN_SCALE = 1.0 / math.sqrt(HEAD_DIM)
GMLP_WIDTH = D_MODEL
GMLP_GROUPS = 4
GMLP_GROUP_DIM = GMLP_WIDTH // GMLP_GROUPS
CHUNK = 128
CONV_WIDTH = D_MODEL
CONV_KERNEL = 31
CONV_PAD = CONV_KERNEL // 2
D_FF = 4 * D_MODEL
EPS = 1e-6
Q_OFF = 0
K_OFF = Q_OFF + ATTN_WIDTH
V_OFF = K_OFF + KV_WIDTH
GU_OFF = V_OFF + KV_WIDTH
GV_OFF = GU_OFF + GMLP_WIDTH
CG_OFF = GV_OFF + GMLP_WIDTH
GATE_OFF = CG_OFF + 2 * CONV_WIDTH
IN_COLS = GATE_OFF + 3 * D_MODEL

LOG2E = 1.4426950408889634
GELU_C = math.sqrt(2.0 / math.pi)
LANES = 128
SUBLANES = 8
HALO = 16
SHIFT_SPAN_EXTRA = SUBLANES * (CONV_KERNEL // SUBLANES)
SUM_ROWS = 16
V_ROWS = HEAD_DIM + SUM_ROWS
FLASH_UNROLL = 8
MOD_ROWS = 8
VMEM_LIMIT = 56 * 1024 * 1024

F32 = jnp.float32
BF16 = jnp.bfloat16


def _sigmoid(x):
    return 1.0 / (1.0 + jnp.exp(-x))


def _gelu_tanh(x):
    return x * (0.5 * (1.0 + jnp.tanh(GELU_C * (x + 0.044715 * (x * x * x)))))


def _rms(x, g):
    return x * lax.rsqrt(jnp.mean(x * x, axis=-1, keepdims=True) + EPS) * g


def _const_spec(shape):
    n = len(shape)
    return pl.BlockSpec(shape, lambda *_: (0,) * n, pipeline_mode=pl.Buffered(1))


def _mod_kernel(cc_ref, w_ref, b_ref, o_ref):
    cc = cc_ref[...]
    a = cc * _sigmoid(cc)
    o_ref[...] = jnp.dot(a, w_ref[...], preferred_element_type=F32,
                         precision=lax.Precision.HIGHEST) + b_ref[...]


def _mod_call(cc, ada_w, ada_b):
    nblk = 4
    bw = 6 * D_MODEL // nblk
    return pl.pallas_call(
        _mod_kernel,
        out_shape=jax.ShapeDtypeStruct((DEPTH, MOD_ROWS, 6 * D_MODEL), F32),
        grid=(DEPTH, nblk),
        in_specs=[
            pl.BlockSpec((MOD_ROWS, D_MODEL), lambda l, j: (0, 0)),
            pl.BlockSpec((None, D_MODEL, bw), lambda l, j: (l, 0, j)),
            pl.BlockSpec((None, 1, bw), lambda l, j: (l, 0, j)),
        ],
        out_specs=pl.BlockSpec((None, MOD_ROWS, bw), lambda l, j: (l, 0, j)),
        compiler_params=pltpu.CompilerParams(
            dimension_semantics=("parallel", "parallel"), vmem_limit_bytes=VMEM_LIMIT),
        name="adaln_table",
    )(cc, ada_w, ada_b.reshape(DEPTH, 1, 6 * D_MODEL))


def _inproj_kernel(*refs, tm, row, rope):
    if rope:
        (x_ref, mod_ref, pre_g_ref, w_ref, qg_ref, kg_ref, cos_ref, sin_ref, gng_ref, ws_ref, bst_ref,
         wgo_ref, q_ref, k_ref, v_ref, gl_ref, u_ref, ga_ref, gc_ref, h_sc, gp_sc) = refs
    else:
        (x_ref, mod_ref, pre_g_ref, w_ref, qg_ref, kg_ref, gng_ref, ws_ref, bst_ref,
         wgo_ref, q_ref, k_ref, v_ref, gl_ref, u_ref, ga_ref, gc_ref, h_sc, gp_sc) = refs
        cos_ref = sin_ref = None

    sh = mod_ref[row:row + 1, 0:D_MODEL]
    sc = mod_ref[row:row + 1, D_MODEL:2 * D_MODEL]
    h = _rms(x_ref[...], pre_g_ref[...]) * (1.0 + sc) + sh
    h_sc[...] = h.astype(BF16)

    def proj(lo, hi):
        return jnp.dot(h_sc[...], w_ref[:, lo:hi], preferred_element_type=F32)

    def head_norm_rope(t, g):
        t = _rms(t, g)
        if rope:
            t = t * cos_ref[...] + pltpu.roll(t, HEAD_DIM // 2, 1) * sin_ref[...]
        return t

    pq = proj(Q_OFF, K_OFF)
    for hh in range(N_Q_HEADS):
        t = head_norm_rope(pq[:, hh * HEAD_DIM:(hh + 1) * HEAD_DIM], qg_ref[...])
        q_ref[hh] = (t * (ATTN_SCALE * LOG2E)).T.astype(BF16)

    pk = proj(K_OFF, V_OFF)
    for hh in range(N_KV_HEADS):
        k_ref[hh] = head_norm_rope(pk[:, hh * HEAD_DIM:(hh + 1) * HEAD_DIM], kg_ref[...]).astype(BF16)
    pv = proj(V_OFF, GU_OFF)
    for hh in range(N_KV_HEADS):
        v_ref[hh, 0:HEAD_DIM, :] = pv[:, hh * HEAD_DIM:(hh + 1) * HEAD_DIM].T.astype(BF16)
        v_ref[hh, HEAD_DIM:HEAD_DIM + SUM_ROWS, :] = jnp.ones((SUM_ROWS, tm), BF16)

    gu = _gelu_tanh(proj(GU_OFF, GV_OFF))
    gv = _rms(_gelu_tanh(proj(GV_OFF, CG_OFF)), gng_ref[...]).astype(BF16)
    for c in range(tm // CHUNK):
        rs = slice(c * CHUNK, (c + 1) * CHUNK)
        for g in range(GMLP_GROUPS):
            cs = slice(g * GMLP_GROUP_DIM, (g + 1) * GMLP_GROUP_DIM)
            sv = jnp.dot(ws_ref[g], gv[rs, cs], preferred_element_type=F32) + bst_ref[:, g:g + 1]
            gp_sc[rs, cs] = (gu[rs, cs] * sv).astype(BF16)
    g_out = jnp.dot(gp_sc[...], wgo_ref[...], preferred_element_type=F32)

    pg = proj(GATE_OFF, IN_COLS)
    ga_ref[...] = _sigmoid(pg[:, 0:D_MODEL]).astype(BF16)
    u_ref[...] = (_sigmoid(pg[:, D_MODEL:2 * D_MODEL]) * g_out).astype(BF16)
    gc_ref[...] = _sigmoid(pg[:, 2 * D_MODEL:3 * D_MODEL]).astype(BF16)

    pa = proj(CG_OFF, CG_OFF + CONV_WIDTH)
    pb = proj(CG_OFF + CONV_WIDTH, GATE_OFF)
    gl_ref[...] = (pa * _sigmoid(pb)).astype(BF16)


def _inproj_call(x, mod, pre_g, w_in, qg, kg, cos2, sin2, gng, ws, bst, wgo, *, tm, row):
    n_tok = x.shape[0]
    rope = cos2 is not None
    row_spec = pl.BlockSpec((tm, D_MODEL), lambda i: (i, 0))
    head_tab = pl.BlockSpec((tm, HEAD_DIM), lambda i: (i, 0))
    in_specs = [row_spec, _const_spec((MOD_ROWS, 6 * D_MODEL)), _const_spec((1, D_MODEL)),
                _const_spec((D_MODEL, IN_COLS)), _const_spec((1, HEAD_DIM)), _const_spec((1, HEAD_DIM))]
    args = [x, mod, pre_g, w_in, qg, kg]
    if rope:
        in_specs += [head_tab, head_tab]
        args += [cos2, sin2]
    in_specs += [_const_spec((1, D_MODEL)), _const_spec((GMLP_GROUPS, CHUNK, CHUNK)),
                 _const_spec((CHUNK, GMLP_GROUPS)), _const_spec((D_MODEL, D_MODEL))]
    args += [gng, ws, bst, wgo]

    def heads(n):
        return (jax.ShapeDtypeStruct((n, n_tok, HEAD_DIM), BF16),
                pl.BlockSpec((n, tm, HEAD_DIM), lambda i: (0, i, 0)))

    def heads_t(n, rows):
        return (jax.ShapeDtypeStruct((n, rows, n_tok), BF16),
                pl.BlockSpec((n, rows, tm), lambda i: (0, 0, i)))

    wide = (jax.ShapeDtypeStruct((n_tok, D_MODEL), BF16), row_spec)
    outs = [heads_t(N_Q_HEADS, HEAD_DIM), heads(N_KV_HEADS), heads_t(N_KV_HEADS, V_ROWS),
            wide, wide, wide, wide]
    return pl.pallas_call(
        functools.partial(_inproj_kernel, tm=tm, row=row, rope=rope),
        out_shape=[o[0] for o in outs],
        grid=(n_tok // tm,),
        in_specs=in_specs,
        out_specs=[o[1] for o in outs],
        scratch_shapes=[pltpu.VMEM((tm, D_MODEL), BF16), pltpu.VMEM((tm, D_MODEL), BF16)],
        compiler_params=pltpu.CompilerParams(
            dimension_semantics=("parallel",), vmem_limit_bytes=VMEM_LIMIT),
        name="inproj_rope" if rope else "inproj_ctx",
    )(*args)


def _flash_kernel(*refs, tq, tk, n_chunks):
    if n_chunks:
        qt_ref, kc_ref, vct_ref, k_ref, vt_ref, o_ref, m_sc, acc_sc, st_sc = refs
    else:
        qt_ref, kc_ref, vct_ref, o_ref, m_sc, acc_sc = refs

    def block(g, k, vt, first, slot=0):
        if first:
            st = jnp.dot(k, qt_ref[g], preferred_element_type=F32)
        else:
            st = st_sc[slot, g]
        m_cur = jnp.max(st, axis=0, keepdims=True)
        if first:
            m_new = m_cur
        else:
            m_prev = m_sc[g]
            m_new = jnp.maximum(m_prev, m_cur)
            alpha = jnp.exp2(m_prev - m_new)
        pt = jnp.exp2(st - m_new)
        pv = jnp.dot(vt, pt.astype(BF16), preferred_element_type=F32)
        if first:
            acc_sc[g] = pv
        else:
            acc_sc[g] = alpha * acc_sc[g] + pv
        m_sc[g] = m_new

    for g in range(GQA_GROUP):
        block(g, kc_ref[...], vct_ref[...], True)

    if n_chunks:
        for g in range(GQA_GROUP):
            st_sc[0, g] = jnp.dot(k_ref[pl.ds(0, tk), :], qt_ref[g], preferred_element_type=F32)

        def stage(c, slot):
            off = pl.multiple_of(c * tk, tk)
            off_next = pl.multiple_of(jnp.minimum(c + 1, n_chunks - 1) * tk, tk)
            vt = vt_ref[:, pl.ds(off, tk)]
            k_next = k_ref[pl.ds(off_next, tk), :]
            for g in range(GQA_GROUP):
                st_sc[1 - slot, g] = jnp.dot(k_next, qt_ref[g], preferred_element_type=F32)
                block(g, None, vt, False, slot)

        def body(j, carry):
            for u in range(FLASH_UNROLL):
                stage(FLASH_UNROLL * j + u, u % 2)
            return carry

        lax.fori_loop(0, n_chunks // FLASH_UNROLL, body, 0)

    for g in range(GQA_GROUP):
        o = acc_sc[g, 0:HEAD_DIM, :] / acc_sc[g, HEAD_DIM:HEAD_DIM + 1, :]
        o_ref[:, g * HEAD_DIM:(g + 1) * HEAD_DIM] = o.T.astype(BF16)


def _flash_call(qt, kc, vct, k, vt, *, tq, tk):
    lq = qt.shape[2]
    lc = kc.shape[1]
    n_chunks = 0 if k is None else k.shape[1] // tk
    assert n_chunks % FLASH_UNROLL == 0
    in_specs = [pl.BlockSpec((GQA_GROUP, HEAD_DIM, tq), lambda h, i: (h, 0, i)),
                pl.BlockSpec((None, lc, HEAD_DIM), lambda h, i: (h, 0, 0)),
                pl.BlockSpec((None, V_ROWS, lc), lambda h, i: (h, 0, 0))]
    args = [qt, kc, vct]
    if n_chunks:
        s_len = k.shape[1]
        in_specs += [pl.BlockSpec((None, s_len, HEAD_DIM), lambda h, i: (h, 0, 0)),
                     pl.BlockSpec((None, V_ROWS, s_len), lambda h, i: (h, 0, 0))]
        args += [k, vt]
    return pl.pallas_call(
        functools.partial(_flash_kernel, tq=tq, tk=tk, n_chunks=n_chunks),
        out_shape=jax.ShapeDtypeStruct((lq, ATTN_WIDTH), BF16),
        grid=(N_KV_HEADS, lq // tq),
        in_specs=in_specs,
        out_specs=pl.BlockSpec((tq, GQA_GROUP * HEAD_DIM), lambda h, i: (i, h)),
        scratch_shapes=[pltpu.VMEM((GQA_GROUP, 1, tq), F32), pltpu.VMEM((GQA_GROUP, V_ROWS, tq), F32)]
        + ([pltpu.VMEM((2, GQA_GROUP, tk, tq), F32)] if n_chunks else []),
        compiler_params=pltpu.CompilerParams(
            dimension_semantics=("parallel", "parallel"), vmem_limit_bytes=VMEM_LIMIT),
        name="flash_latent" if n_chunks else "flash_ctx",
    )(*args)


def _post_kernel(x_ref, attn_ref, glp_ref, gl_ref, gln_ref, u_ref, ga_ref, gc_ref, mod_ref,
                 post_g_ref, fpre_g_ref, fpost_g_ref, cw_ref, cb_ref, lng_ref, lnb_ref,
                 wao_ref, wco_ref, wout_ref, w1_ref, w2_ref, xo_ref, win_sc, shift_sc, conv_sc, *, tm, row):
    i = pl.program_id(0)
    n = pl.num_programs(0)
    span = tm + SHIFT_SPAN_EXTRA
    rb = 32

    keep_prev = jnp.where(i > 0, 1.0, 0.0).astype(F32)
    keep_next = jnp.where(i < n - 1, 1.0, 0.0).astype(F32)
    win_sc[0:HALO, :] = glp_ref[...].astype(F32) * keep_prev
    win_sc[HALO:HALO + tm, :] = gl_ref[...].astype(F32)
    win_sc[HALO + tm:HALO + tm + HALO, :] = gln_ref[...].astype(F32) * keep_next
    for cblk in range(D_MODEL // LANES):
        cs = slice(cblk * LANES, (cblk + 1) * LANES)
        buf = cblk % 2
        for s in range(1, SUBLANES):
            shift_sc[buf, s - 1] = win_sc[s:s + span, cs]
        for r in range(tm // rb):
            acc = jnp.broadcast_to(cb_ref[:, cs], (rb, LANES))
            for d in range(1, CONV_KERNEL + 1):
                a, s = divmod(d, SUBLANES)
                base = r * rb + SUBLANES * a
                if s == 0:
                    src = win_sc[base:base + rb, cs]
                else:
                    src = shift_sc[buf, s - 1, base:base + rb, :]
                acc = acc + cw_ref[d - 1:d, cs] * src
            conv_sc[r * rb:(r + 1) * rb, cs] = acc
    y = conv_sc[...]
    mu = jnp.mean(y, axis=-1, keepdims=True)
    yc = y - mu
    var = jnp.mean(yc * yc, axis=-1, keepdims=True)
    y = yc * lax.rsqrt(var + EPS) * lng_ref[...] + lnb_ref[...]
    y = y * _sigmoid(y)
    c_out = jnp.dot(y.astype(BF16), wco_ref[...], preferred_element_type=F32)

    a_out = jnp.dot(attn_ref[...], wao_ref[...], preferred_element_type=F32)
    merged = (ga_ref[...].astype(F32) * a_out + u_ref[...].astype(F32)
              + gc_ref[...].astype(F32) * c_out)
    ymix = jnp.dot(merged.astype(BF16), wout_ref[...], preferred_element_type=F32)

    def mod(j):
        return mod_ref[row:row + 1, j * D_MODEL:(j + 1) * D_MODEL]

    x1 = x_ref[...] + mod(2) * _rms(ymix, post_g_ref[...])
    h2 = (_rms(x1, fpre_g_ref[...]) * (1.0 + mod(4)) + mod(3)).astype(BF16)
    f = jnp.dot(h2, w1_ref[...], preferred_element_type=F32)
    f = jnp.maximum(f, 0.0)
    f = (f * f).astype(BF16)
    f2 = jnp.dot(f, w2_ref[...], preferred_element_type=F32)
    xo_ref[...] = x1 + mod(5) * _rms(f2, fpost_g_ref[...])


def _post_call(x, attn, gl, u, ga, gc, mod, post_g, fpre_g, fpost_g, cw, cb, lng, lnb,
               wao, wco, wout, w1, w2, *, tm, row):
    n_tok = x.shape[0]
    hb = tm // HALO
    n_hb = n_tok // HALO
    row_spec = pl.BlockSpec((tm, D_MODEL), lambda i: (i, 0))
    prev_spec = pl.BlockSpec((HALO, D_MODEL), lambda i: (jnp.maximum(i * hb - 1, 0), 0))
    next_spec = pl.BlockSpec((HALO, D_MODEL), lambda i: (jnp.minimum((i + 1) * hb, n_hb - 1), 0))
    vec = _const_spec((1, D_MODEL))
    sq = _const_spec((D_MODEL, D_MODEL))
    in_specs = [row_spec, row_spec, prev_spec, row_spec, next_spec, row_spec, row_spec, row_spec,
                _const_spec((MOD_ROWS, 6 * D_MODEL)), vec, vec, vec,
                _const_spec((CONV_KERNEL, D_MODEL)), vec, vec, vec,
                sq, sq, sq, _const_spec((D_MODEL, D_FF)), _const_spec((D_FF, D_MODEL))]
    return pl.pallas_call(
        functools.partial(_post_kernel, tm=tm, row=row),
        out_shape=jax.ShapeDtypeStruct((n_tok, D_MODEL), F32),
        grid=(n_tok // tm,),
        in_specs=in_specs,
        out_specs=row_spec,
        scratch_shapes=[pltpu.VMEM((tm + 2 * HALO, D_MODEL), F32),
                        pltpu.VMEM((2, SUBLANES - 1, tm + SHIFT_SPAN_EXTRA, LANES), F32),
                        pltpu.VMEM((tm, D_MODEL), F32)],
        compiler_params=pltpu.CompilerParams(
            dimension_semantics=("parallel",), vmem_limit_bytes=VMEM_LIMIT),
        name="merge_mlp",
    )(x, attn, gl, gl, gl, u, ga, gc, mod, post_g, fpre_g, fpost_g, cw, cb, lng, lnb,
      wao, wco, wout, w1, w2)


def _rope_tables(n_rows):
    row = jnp.repeat(jnp.arange(n_rows), GRID_W).astype(F32)
    col = jnp.tile(jnp.arange(GRID_W), n_rows).astype(F32)
    inv = ROPE_THETA ** (-jnp.arange(ROPE_FREQS, dtype=F32) / ROPE_FREQS)
    ang = jnp.concatenate([row[:, None] * inv, col[:, None] * inv], axis=-1)
    cos, sin = jnp.cos(ang), jnp.sin(ang)
    return jnp.concatenate([cos, cos], axis=-1), jnp.concatenate([-sin, sin], axis=-1)


def kernel(x, c, ctx, c_ctx, ada_w, ada_b, mix_pre_g, mix_post_g, w_in, q_norm_g, k_norm_g, w_attn_o,
           gmlp_norm_g, gmlp_ws, gmlp_bs, w_gmlp_o, conv_w, conv_b, conv_norm_g, conv_norm_b, w_conv_o,
           w_out, ffn_pre_g, ffn_post_g, w_ff1, w_ff2):
    batch, seq, _ = x.shape
    assert batch == 1 and c.shape[0] == 1 and ctx.shape[0] == 1
    n_ctx = ctx.shape[1]
    tm_lat, tm_ctx = 256, n_ctx
    tq, tk = 512, 512
    assert seq % GRID_W == 0 and seq % tm_lat == 0 and seq % tk == 0 and seq % tq == 0 and n_ctx % CHUNK == 0

    xl, xc = x[0], ctx[0]
    cc = jnp.zeros((MOD_ROWS, D_MODEL), F32).at[0].set(c[0]).at[1].set(c_ctx)
    mods = _mod_call(cc, ada_w, ada_b)
    cos2, sin2 = _rope_tables(seq // GRID_W)

    def vec(a):
        return a.reshape(1, -1)

    for l in range(DEPTH):
        last = l == DEPTH - 1
        mod = mods[l]
        inproj_w = (vec(mix_pre_g[l]), w_in[l].astype(BF16), vec(q_norm_g[l]), vec(k_norm_g[l]))
        gmlp_w = (vec(gmlp_norm_g[l]), gmlp_ws[l].astype(BF16), gmlp_bs[l].T, w_gmlp_o[l].astype(BF16))
        post_w = (vec(mix_post_g[l]), vec(ffn_pre_g[l]), vec(ffn_post_g[l]), conv_w[l], vec(conv_b[l]),
                  vec(conv_norm_g[l]), vec(conv_norm_b[l]), w_attn_o[l].astype(BF16),
                  w_conv_o[l].astype(BF16), w_out[l].astype(BF16), w_ff1[l].astype(BF16),
                  w_ff2[l].astype(BF16))

        qt, k, vt, gl, u, ga, gc = _inproj_call(xl, mod, *inproj_w, cos2, sin2, *gmlp_w, tm=tm_lat, row=0)
        qct, kc, vct, glc, uc, gac, gcc = _inproj_call(xc, mod, *inproj_w, None, None, *gmlp_w,
                                                       tm=tm_ctx, row=1)
        attn = _flash_call(qt, kc, vct, k, vt, tq=tq, tk=tk)
        if not last:
            attn_c = _flash_call(qct, kc, vct, None, None, tq=n_ctx, tk=tk)
            xc = _post_call(xc, attn_c, glc, uc, gac, gcc, mod, *post_w, tm=tm_ctx, row=1)
        xl = _post_call(xl, attn, gl, u, ga, gc, mod, *post_w, tm=tm_lat, row=0)
    return xl[None]
```

```python
import functools
import math

import jax
import jax.numpy as jnp
from jax import lax
from jax.experimental import pallas as pl
from jax.experimental.pallas import tpu as pltpu

D_MODEL = 1024
DEPTH = 2
GRID_W = 64
N_Q_HEADS = 8
N_KV_HEADS = 2
GQA_GROUP = N_Q_HEADS // N_KV_HEADS
HEAD_DIM = 128
ATTN_WIDTH = N_Q_HEADS * HEAD_DIM
KV_WIDTH = N_KV_HEADS * HEAD_DIM
ROPE_THETA = 10000.0
ROPE_FREQS = HEAD_DIM // 4
ATTN_SCALE = 1.0 / math.sqrt(HEAD_DIM)
GMLP_WIDTH = D_MODEL
GMLP_GROUPS = 4
GMLP_GROUP_DIM = GMLP_WIDTH // GMLP_GROUPS
CHUNK = 128
CONV_WIDTH = D_MODEL
CONV_KERNEL = 31
CONV_PAD = CONV_KERNEL // 2
D_FF = 4 * D_MODEL
EPS = 1e-6
Q_OFF = 0
K_OFF = Q_OFF + ATTN_WIDTH
V_OFF = K_OFF + KV_WIDTH
GU_OFF = V_OFF + KV_WIDTH
GV_OFF = GU_OFF + GMLP_WIDTH
CG_OFF = GV_OFF + GMLP_WIDTH
GATE_OFF = CG_OFF + 2 * CONV_WIDTH
IN_COLS = GATE_OFF + 3 * D_MODEL

LOG2E = 1.4426950408889634
GELU_C = math.sqrt(2.0 / math.pi)
LANES = 128
SUBLANES = 8
HALO = 16
SHIFT_SPAN_EXTRA = SUBLANES * (CONV_KERNEL // SUBLANES)
SUM_ROWS = 16
V_ROWS = HEAD_DIM + SUM_ROWS
FLASH_UNROLL = 8
MOD_ROWS = 8
VMEM_LIMIT = 56 * 1024 * 1024

F32 = jnp.float32
BF16 = jnp.bfloat16


def _sigmoid(x):
    return 1.0 / (1.0 + jnp.exp(-x))


def _gelu_tanh(x):
    return x * (0.5 * (1.0 + jnp.tanh(GELU_C * (x + 0.044715 * (x * x * x)))))


def _rms(x, g):
    return x * lax.rsqrt(jnp.mean(x * x, axis=-1, keepdims=True) + EPS) * g


def _const_spec(shape):
    n = len(shape)
    return pl.BlockSpec(shape, lambda *_: (0,) * n, pipeline_mode=pl.Buffered(1))


def _mod_kernel(cc_ref, w_ref, b_ref, o_ref):
    cc = cc_ref[...]
    a = cc * _sigmoid(cc)
    o_ref[...] = jnp.dot(a, w_ref[...], preferred_element_type=F32,
                         precision=lax.Precision.HIGHEST) + b_ref[...]


def _mod_call(cc, ada_w, ada_b):
    nblk = 4
    bw = 6 * D_MODEL // nblk
    return pl.pallas_call(
        _mod_kernel,
        out_shape=jax.ShapeDtypeStruct((DEPTH, MOD_ROWS, 6 * D_MODEL), F32),
        grid=(DEPTH, nblk),
        in_specs=[
            pl.BlockSpec((MOD_ROWS, D_MODEL), lambda l, j: (0, 0)),
            pl.BlockSpec((None, D_MODEL, bw), lambda l, j: (l, 0, j)),
            pl.BlockSpec((None, 1, bw), lambda l, j: (l, 0, j)),
        ],
        out_specs=pl.BlockSpec((None, MOD_ROWS, bw), lambda l, j: (l, 0, j)),
        compiler_params=pltpu.CompilerParams(
            dimension_semantics=("parallel", "parallel"), vmem_limit_bytes=VMEM_LIMIT),
        name="adaln_table",
    )(cc, ada_w, ada_b.reshape(DEPTH, 1, 6 * D_MODEL))


def _inproj_kernel(*refs, tm, row, rope):
    if rope:
        (x_ref, mod_ref, pre_g_ref, w_ref, qg_ref, kg_ref, cos_ref, sin_ref, gng_ref, ws_ref, bst_ref,
         wgo_ref, q_ref, k_ref, v_ref, gl_ref, u_ref, ga_ref, gc_ref, h_sc, gp_sc) = refs
    else:
        (x_ref, mod_ref, pre_g_ref, w_ref, qg_ref, kg_ref, gng_ref, ws_ref, bst_ref,
         wgo_ref, q_ref, k_ref, v_ref, gl_ref, u_ref, ga_ref, gc_ref, h_sc, gp_sc) = refs
        cos_ref = sin_ref = None

    sh = mod_ref[row:row + 1, 0:D_MODEL]
    sc = mod_ref[row:row + 1, D_MODEL:2 * D_MODEL]
    h = _rms(x_ref[...], pre_g_ref[...]) * (1.0 + sc) + sh
    h_sc[...] = h.astype(BF16)

    def proj(lo, hi):
        return jnp.dot(h_sc[...], w_ref[:, lo:hi], preferred_element_type=F32)

    def head_norm_rope(t, g):
        t = _rms(t, g)
        if rope:
            t = t * cos_ref[...] + pltpu.roll(t, HEAD_DIM // 2, 1) * sin_ref[...]
        return t

    pq = proj(Q_OFF, K_OFF)
    for hh in range(N_Q_HEADS):
        t = head_norm_rope(pq[:, hh * HEAD_DIM:(hh + 1) * HEAD_DIM], qg_ref[...])
        q_ref[hh] = (t * (ATTN_SCALE * LOG2E)).T.astype(BF16)

    pk = proj(K_OFF, V_OFF)
    for hh in range(N_KV_HEADS):
        k_ref[hh] = head_norm_rope(pk[:, hh * HEAD_DIM:(hh + 1) * HEAD_DIM], kg_ref[...]).astype(BF16)
    pv = proj(V_OFF, GU_OFF)
    for hh in range(N_KV_HEADS):
        v_ref[hh, 0:HEAD_DIM, :] = pv[:, hh * HEAD_DIM:(hh + 1) * HEAD_DIM].T.astype(BF16)
        v_ref[hh, HEAD_DIM:HEAD_DIM + SUM_ROWS, :] = jnp.ones((SUM_ROWS, tm), BF16)

    gu = _gelu_tanh(proj(GU_OFF, GV_OFF))
    gv = _rms(_gelu_tanh(proj(GV_OFF, CG_OFF)), gng_ref[...]).astype(BF16)
    for c in range(tm // CHUNK):
        rs = slice(c * CHUNK, (c + 1) * CHUNK)
        for g in range(GMLP_GROUPS):
            cs = slice(g * GMLP_GROUP_DIM, (g + 1) * GMLP_GROUP_DIM)
            sv = jnp.dot(ws_ref[g], gv[rs, cs], preferred_element_type=F32) + bst_ref[:, g:g + 1]
            gp_sc[rs, cs] = (gu[rs, cs] * sv).astype(BF16)
    g_out = jnp.dot(gp_sc[...], wgo_ref[...], preferred_element_type=F32)

    pg = proj(GATE_OFF, IN_COLS)
    ga_ref[...] = _sigmoid(pg[:, 0:D_MODEL]).astype(BF16)
    u_ref[...] = (_sigmoid(pg[:, D_MODEL:2 * D_MODEL]) * g_out).astype(BF16)
    gc_ref[...] = _sigmoid(pg[:, 2 * D_MODEL:3 * D_MODEL]).astype(BF16)

    pa = proj(CG_OFF, CG_OFF + CONV_WIDTH)
    pb = proj(CG_OFF + CONV_WIDTH, GATE_OFF)
    gl_ref[...] = (pa * _sigmoid(pb)).astype(BF16)


def _inproj_call(x, mod, pre_g, w_in, qg, kg, cos2, sin2, gng, ws, bst, wgo, *, tm, row):
    n_tok = x.shape[0]
    rope = cos2 is not None
    row_spec = pl.BlockSpec((tm, D_MODEL), lambda i: (i, 0))
    head_tab = pl.BlockSpec((tm, HEAD_DIM), lambda i: (i, 0))
    in_specs = [row_spec, _const_spec((MOD_ROWS, 6 * D_MODEL)), _const_spec((1, D_MODEL)),
                _const_spec((D_MODEL, IN_COLS)), _const_spec((1, HEAD_DIM)), _const_spec((1, HEAD_DIM))]
    args = [x, mod, pre_g, w_in, qg, kg]
    if rope:
        in_specs += [head_tab, head_tab]
        args += [cos2, sin2]
    in_specs += [_const_spec((1, D_MODEL)), _const_spec((GMLP_GROUPS, CHUNK, CHUNK)),
                 _const_spec((CHUNK, GMLP_GROUPS)), _const_spec((D_MODEL, D_MODEL))]
    args += [gng, ws, bst, wgo]

    def heads(n):
        return (jax.ShapeDtypeStruct((n, n_tok, HEAD_DIM), BF16),
                pl.BlockSpec((n, tm, HEAD_DIM), lambda i: (0, i, 0)))

    def heads_t(n, rows):
        return (jax.ShapeDtypeStruct((n, rows, n_tok), BF16),
                pl.BlockSpec((n, rows, tm), lambda i: (0, 0, i)))

    wide = (jax.ShapeDtypeStruct((n_tok, D_MODEL), BF16), row_spec)
    outs = [heads_t(N_Q_HEADS, HEAD_DIM), heads(N_KV_HEADS), heads_t(N_KV_HEADS, V_ROWS),
            wide, wide, wide, wide]
    return pl.pallas_call(
        functools.partial(_inproj_kernel, tm=tm, row=row, rope=rope),
        out_shape=[o[0] for o in outs],
        grid=(n_tok // tm,),
        in_specs=in_specs,
        out_specs=[o[1] for o in outs],
        scratch_shapes=[pltpu.VMEM((tm, D_MODEL), BF16), pltpu.VMEM((tm, D_MODEL), BF16)],
        compiler_params=pltpu.CompilerParams(
            dimension_semantics=("parallel",), vmem_limit_bytes=VMEM_LIMIT),
        name="inproj_rope" if rope else "inproj_ctx",
    )(*args)


def _flash_kernel(*refs, tq, tk, n_chunks):
    if n_chunks:
        qt_ref, kc_ref, vct_ref, k_ref, vt_ref, o_ref, m_sc, acc_sc, st_sc = refs
    else:
        qt_ref, kc_ref, vct_ref, o_ref, m_sc, acc_sc = refs

    def block(g, k, vt, first, slot=0):
        if first:
            st = jnp.dot(k, qt_ref[g], preferred_element_type=F32)
        else:
            st = st_sc[slot, g]
        m_cur = jnp.max(st, axis=0, keepdims=True)
        if first:
            m_new = m_cur
        else:
            m_prev = m_sc[g]
            m_new = jnp.maximum(m_prev, m_cur)
            alpha = jnp.exp2(m_prev - m_new)
        pt = jnp.exp2(st - m_new)
        pv = jnp.dot(vt, pt.astype(BF16), preferred_element_type=F32)
        if first:
            acc_sc[g] = pv
        else:
            acc_sc[g] = alpha * acc_sc[g] + pv
        m_sc[g] = m_new

    for g in range(GQA_GROUP):
        if n_chunks:
            st_sc[0, g] = jnp.dot(k_ref[pl.ds(0, tk), :], qt_ref[g], preferred_element_type=F32)
        block(g, kc_ref[...], vct_ref[...], True)

    if n_chunks:

        def stage(c, slot):
            off = pl.multiple_of(c * tk, tk)
            off_next = pl.multiple_of(jnp.minimum(c + 1, n_chunks - 1) * tk, tk)
            vt = vt_ref[:, pl.ds(off, tk)]
            k_next = k_ref[pl.ds(off_next, tk), :]
            for g in range(GQA_GROUP):
                st_sc[1 - slot, g] = jnp.dot(k_next, qt_ref[g], preferred_element_type=F32)
                block(g, None, vt, False, slot)

        def body(j, carry):
            for u in range(FLASH_UNROLL):
                stage(FLASH_UNROLL * j + u, u % 2)
            return carry

        lax.fori_loop(0, n_chunks // FLASH_UNROLL, body, 0)

    for g in range(GQA_GROUP):
        o = acc_sc[g, 0:HEAD_DIM, :] / acc_sc[g, HEAD_DIM:HEAD_DIM + 1, :]
        o_ref[:, g * HEAD_DIM:(g + 1) * HEAD_DIM] = o.T.astype(BF16)


def _flash_call(qt, kc, vct, k, vt, *, tq, tk):
    lq = qt.shape[2]
    lc = kc.shape[1]
    n_chunks = 0 if k is None else k.shape[1] // tk
    assert n_chunks % FLASH_UNROLL == 0
    in_specs = [pl.BlockSpec((GQA_GROUP, HEAD_DIM, tq), lambda h, i: (h, 0, i)),
                pl.BlockSpec((None, lc, HEAD_DIM), lambda h, i: (h, 0, 0)),
                pl.BlockSpec((None, V_ROWS, lc), lambda h, i: (h, 0, 0))]
    args = [qt, kc, vct]
    if n_chunks:
        s_len = k.shape[1]
        in_specs += [pl.BlockSpec((None, s_len, HEAD_DIM), lambda h, i: (h, 0, 0)),
                     pl.BlockSpec((None, V_ROWS, s_len), lambda h, i: (h, 0, 0))]
        args += [k, vt]
    return pl.pallas_call(
        functools.partial(_flash_kernel, tq=tq, tk=tk, n_chunks=n_chunks),
        out_shape=jax.ShapeDtypeStruct((lq, ATTN_WIDTH), BF16),
        grid=(N_KV_HEADS, lq // tq),
        in_specs=in_specs,
        out_specs=pl.BlockSpec((tq, GQA_GROUP * HEAD_DIM), lambda h, i: (i, h)),
        scratch_shapes=[pltpu.VMEM((GQA_GROUP, 1, tq), F32), pltpu.VMEM((GQA_GROUP, V_ROWS, tq), F32)]
        + ([pltpu.VMEM((2, GQA_GROUP, tk, tq), F32)] if n_chunks else []),
        compiler_params=pltpu.CompilerParams(
            dimension_semantics=("parallel", "parallel"), vmem_limit_bytes=VMEM_LIMIT),
        name="flash_latent" if n_chunks else "flash_ctx",
    )(*args)


def _post_kernel(x_ref, attn_ref, glp_ref, gl_ref, gln_ref, u_ref, ga_ref, gc_ref, mod_ref,
                 post_g_ref, fpre_g_ref, fpost_g_ref, cw_ref, cb_ref, lng_ref, lnb_ref,
                 wao_ref, wco_ref, wout_ref, w1_ref, w2_ref, xo_ref, win_sc, shift_sc, conv_sc, *, tm, row):
    i = pl.program_id(0)
    n = pl.num_programs(0)
    span = tm + SHIFT_SPAN_EXTRA
    rb = 32

    a_out = jnp.dot(attn_ref[...], wao_ref[...], preferred_element_type=F32)

    keep_prev = jnp.where(i > 0, 1.0, 0.0).astype(F32)
    keep_next = jnp.where(i < n - 1, 1.0, 0.0).astype(F32)
    win_sc[0:HALO, :] = glp_ref[...].astype(F32) * keep_prev
    win_sc[HALO:HALO + tm, :] = gl_ref[...].astype(F32)
    win_sc[HALO + tm:HALO + tm + HALO, :] = gln_ref[...].astype(F32) * keep_next
    for cblk in range(D_MODEL // LANES):
        cs = slice(cblk * LANES, (cblk + 1) * LANES)
        buf = cblk % 2
        for s in range(1, SUBLANES):
            shift_sc[buf, s - 1] = win_sc[s:s + span, cs]
        for r in range(tm // rb):
            acc = jnp.broadcast_to(cb_ref[:, cs], (rb, LANES))
            for d in range(1, CONV_KERNEL + 1):
                a, s = divmod(d, SUBLANES)
                base = r * rb + SUBLANES * a
                if s == 0:
                    src = win_sc[base:base + rb, cs]
                else:
                    src = shift_sc[buf, s - 1, base:base + rb, :]
                acc = acc + cw_ref[d - 1:d, cs] * src
            conv_sc[r * rb:(r + 1) * rb, cs] = acc
    y = conv_sc[...]
    mu = jnp.mean(y, axis=-1, keepdims=True)
    yc = y - mu
    var = jnp.mean(yc * yc, axis=-1, keepdims=True)
    y = yc * lax.rsqrt(var + EPS) * lng_ref[...] + lnb_ref[...]
    y = y * _sigmoid(y)
    c_out = jnp.dot(y.astype(BF16), wco_ref[...], preferred_element_type=F32)
    merged = (ga_ref[...].astype(F32) * a_out + u_ref[...].astype(F32)
              + gc_ref[...].astype(F32) * c_out)
    ymix = jnp.dot(merged.astype(BF16), wout_ref[...], preferred_element_type=F32)

    def mod(j):
        return mod_ref[row:row + 1, j * D_MODEL:(j + 1) * D_MODEL]

    x1 = x_ref[...] + mod(2) * _rms(ymix, post_g_ref[...])
    h2 = (_rms(x1, fpre_g_ref[...]) * (1.0 + mod(4)) + mod(3)).astype(BF16)
    f = jnp.dot(h2, w1_ref[...], preferred_element_type=F32)
    f = jnp.maximum(f, 0.0)
    f = (f * f).astype(BF16)
    f2 = jnp.dot(f, w2_ref[...], preferred_element_type=F32)
    xo_ref[...] = x1 + mod(5) * _rms(f2, fpost_g_ref[...])


def _post_call(x, attn, gl, u, ga, gc, mod, post_g, fpre_g, fpost_g, cw, cb, lng, lnb,
               wao, wco, wout, w1, w2, *, tm, row):
    n_tok = x.shape[0]
    hb = tm // HALO
    n_hb = n_tok // HALO
    row_spec = pl.BlockSpec((tm, D_MODEL), lambda i: (i, 0))
    prev_spec = pl.BlockSpec((HALO, D_MODEL), lambda i: (jnp.maximum(i * hb - 1, 0), 0))
    next_spec = pl.BlockSpec((HALO, D_MODEL), lambda i: (jnp.minimum((i + 1) * hb, n_hb - 1), 0))
    vec = _const_spec((1, D_MODEL))
    sq = _const_spec((D_MODEL, D_MODEL))
    in_specs = [row_spec, row_spec, prev_spec, row_spec, next_spec, row_spec, row_spec, row_spec,
                _const_spec((MOD_ROWS, 6 * D_MODEL)), vec, vec, vec,
                _const_spec((CONV_KERNEL, D_MODEL)), vec, vec, vec,
                sq, sq, sq, _const_spec((D_MODEL, D_FF)), _const_spec((D_FF, D_MODEL))]
    return pl.pallas_call(
        functools.partial(_post_kernel, tm=tm, row=row),
        out_shape=jax.ShapeDtypeStruct((n_tok, D_MODEL), F32),
        grid=(n_tok // tm,),
        in_specs=in_specs,
        out_specs=row_spec,
        scratch_shapes=[pltpu.VMEM((tm + 2 * HALO, D_MODEL), F32),
                        pltpu.VMEM((2, SUBLANES - 1, tm + SHIFT_SPAN_EXTRA, LANES), F32),
                        pltpu.VMEM((tm, D_MODEL), F32)],
        compiler_params=pltpu.CompilerParams(
            dimension_semantics=("parallel",), vmem_limit_bytes=VMEM_LIMIT),
        name="merge_mlp",
    )(x, attn, gl, gl, gl, u, ga, gc, mod, post_g, fpre_g, fpost_g, cw, cb, lng, lnb,
      wao, wco, wout, w1, w2)


def _rope_tables(n_rows):
    row = jnp.repeat(jnp.arange(n_rows), GRID_W).astype(F32)
    col = jnp.tile(jnp.arange(GRID_W), n_rows).astype(F32)
    inv = ROPE_THETA ** (-jnp.arange(ROPE_FREQS, dtype=F32) / ROPE_FREQS)
    ang = jnp.concatenate([row[:, None] * inv, col[:, None] * inv], axis=-1)
    cos, sin = jnp.cos(ang), jnp.sin(ang)
    return jnp.concatenate([cos, cos], axis=-1), jnp.concatenate([-sin, sin], axis=-1)


def kernel(x, c, ctx, c_ctx, ada_w, ada_b, mix_pre_g, mix_post_g, w_in, q_norm_g, k_norm_g, w_attn_o,
           gmlp_norm_g, gmlp_ws, gmlp_bs, w_gmlp_o, conv_w, conv_b, conv_norm_g, conv_norm_b, w_conv_o,
           w_out, ffn_pre_g, ffn_post_g, w_ff1, w_ff2):
    batch, seq, _ = x.shape
    assert batch == 1 and c.shape[0] == 1 and ctx.shape[0] == 1
    n_ctx = ctx.shape[1]
    tm_lat, tm_ctx = 256, n_ctx
    tq, tk = 512, 512
    assert seq % GRID_W == 0 and seq % tm_lat == 0 and seq % tk == 0 and seq % tq == 0 and n_ctx % CHUNK == 0

    xl, xc = x[0], ctx[0]
    cc = jnp.zeros((MOD_ROWS, D_MODEL), F32).at[0].set(c[0]).at[1].set(c_ctx)
    mods = _mod_call(cc, ada_w, ada_b)
    cos2, sin2 = _rope_tables(seq // GRID_W)

    def vec(a):
        return a.reshape(1, -1)

    for l in range(DEPTH):
        last = l == DEPTH - 1
        mod = mods[l]
        inproj_w = (vec(mix_pre_g[l]), w_in[l].astype(BF16), vec(q_norm_g[l]), vec(k_norm_g[l]))
        gmlp_w = (vec(gmlp_norm_g[l]), gmlp_ws[l].astype(BF16), gmlp_bs[l].T, w_gmlp_o[l].astype(BF16))
        post_w = (vec(mix_post_g[l]), vec(ffn_pre_g[l]), vec(ffn_post_g[l]), conv_w[l], vec(conv_b[l]),
                  vec(conv_norm_g[l]), vec(conv_norm_b[l]), w_attn_o[l].astype(BF16),
                  w_conv_o[l].astype(BF16), w_out[l].astype(BF16), w_ff1[l].astype(BF16),
                  w_ff2[l].astype(BF16))

        qt, k, vt, gl, u, ga, gc = _inproj_call(xl, mod, *inproj_w, cos2, sin2, *gmlp_w, tm=tm_lat, row=0)
        qct, kc, vct, glc, uc, gac, gcc = _inproj_call(xc, mod, *inproj_w, None, None, *gmlp_w,
                                                       tm=tm_ctx, row=1)
        attn = _flash_call(qt, kc, vct, k, vt, tq=tq, tk=tk)
        if not last:
            attn_c = _flash_call(qct, kc, vct, None, None, tq=n_ctx, tk=tk)
            xc = _post_call(xc, attn_c, glc, uc, gac, gcc, mod, *post_w, tm=tm_ctx, row=1)
        xl = _post_call(xl, attn, gl, u, ga, gc, mod, *post_w, tm=tm_lat, row=0)
    return xl[None]
```
